```python
import math
import jax, jax.numpy as jnp
from jax import lax
import numpy as np

D_MODEL = 1024
BATCH = 16
SEQ = 2048
DEPTH = 4

GRID_W = 64
CTX_LEN = 256

HG_HEADS = 8
HG_DK = 128
HG_DV = 128
HG_WIDTH = HG_HEADS * HG_DK
HG_CHUNK = 32

AT_HEADS = 8
AT_KV_HEADS = 2
AT_HEAD_DIM = 128
AT_GROUP = AT_HEADS // AT_KV_HEADS
AT_WIDTH = AT_HEADS * AT_HEAD_DIM
AT_KV_WIDTH = AT_KV_HEADS * AT_HEAD_DIM
AT_BLOCK = 128
ROPE_THETA = 10000.0

MB_INNER = 2 * D_MODEL
MB_HEAD_DIM = 64
MB_HEADS = MB_INNER // MB_HEAD_DIM
MB_STATE = 128
MB_GROUPS = 4
MB_HEADS_PER_GROUP = MB_HEADS // MB_GROUPS
MB_CONV = 5
MB_CHUNK = 128
MB_BC_WIDTH = MB_GROUPS * MB_STATE
MB_CONV_DIM = MB_INNER + 2 * MB_BC_WIDTH

FFN_HIDDEN = ((8 * D_MODEL + 3 * 256 - 1) // (3 * 256)) * 256

IN_SIZES = (HG_WIDTH,) * 5 + (AT_WIDTH, AT_KV_WIDTH, AT_KV_WIDTH) + (MB_INNER, MB_CONV_DIM, 2 * MB_HEADS) + (3 * D_MODEL,)
IN_WIDTH = sum(IN_SIZES)
IN_SPLITS = tuple(int(s) for s in np.cumsum(IN_SIZES)[:-1])

DN_ALPHA = (2 * DEPTH) ** 0.25
DN_BETA = (8 * DEPTH) ** -0.25
LN_EPS = 1e-5
RMS_EPS = 1e-6
F32 = jnp.float32

kernel_name = "hybrid_hgrn2_gqa_ssd_dit_prefix"


def layer_norm(x, g, b):
    xf = x.astype(F32)
    mu = jnp.mean(xf, axis=-1, keepdims=True)
    var = jnp.mean(jnp.square(xf - mu), axis=-1, keepdims=True)
    return ((xf - mu) * lax.rsqrt(var + LN_EPS) * g + b).astype(x.dtype)


def rms_norm(x, w):
    xf = x.astype(F32)
    return (xf * lax.rsqrt(jnp.mean(xf * xf, axis=-1, keepdims=True) + RMS_EPS) * w).astype(x.dtype)


def flip(a):
    return jnp.flip(a, axis=1)


def to_chunks(a, size):
    n, t = a.shape[:2]
    return jnp.moveaxis(a.reshape(n, t // size, size, *a.shape[2:]), 1, 0)


def from_chunks(a):
    a = jnp.moveaxis(a, 0, 1)
    return a.reshape(a.shape[0], -1, *a.shape[3:])


def axial_rope(n_tokens):
    rows = n_tokens // GRID_W
    row, col = jnp.meshgrid(jnp.arange(rows, dtype=F32), jnp.arange(GRID_W, dtype=F32), indexing="ij")
    n_pairs = AT_HEAD_DIM // 4
    inv_freq = ROPE_THETA ** (-jnp.arange(n_pairs, dtype=F32) / n_pairs)
    ang = jnp.concatenate([row.reshape(-1, 1) * inv_freq, col.reshape(-1, 1) * inv_freq], axis=-1)
    return jnp.cos(ang), jnp.sin(ang)


def apply_rope(x, cos, sin):
    xp = x.astype(F32).reshape(*x.shape[:-1], -1, 2)
    x1, x2 = xp[..., 0], xp[..., 1]
    c, s = cos[None, :, None, :], sin[None, :, None, :]
    out = jnp.stack([x1 * c - x2 * s, x1 * s + x2 * c], axis=-1)
    return out.reshape(x.shape).astype(x.dtype)


def hgrn2_forget(f_pre, lb):
    log_f = jnp.logaddexp(jnp.log(lb), jnp.log1p(-lb) + jax.nn.log_sigmoid(f_pre))
    k = (1.0 - lb) * jax.nn.sigmoid(-f_pre)
    return log_f, k


def hgrn2_inputs(q, f_fwd, f_bwd, i, lb_f, lb_b):
    n, t = q.shape[:2]
    heads = lambda a: a.reshape(n, t, HG_HEADS, -1).astype(F32)
    q = jax.nn.silu(heads(q)) * HG_DK ** -0.5
    lf_f, k_f = hgrn2_forget(heads(f_fwd), lb_f)
    lf_b, k_b = hgrn2_forget(heads(f_bwd), lb_b)
    return q, heads(i), lf_f, k_f, lf_b, k_b


def hgrn2_scan(q, k, v, log_f, state0):
    L = HG_CHUNK
    causal = jnp.tril(jnp.ones((L, L), dtype=bool))[None, :, :, None, None]

    def step(state, blk):
        qc, kc, vc, gc = blk
        b = jnp.cumsum(gc, axis=1)
        diff = b[:, :, None] - b[:, None, :]
        decay = jnp.exp(jnp.where(causal, diff, -jnp.inf))
        attn = jnp.einsum("bthk,bshk,btshk->bhts", qc, kc, decay)
        o = jnp.einsum("bhts,bshv->bthv", attn, vc) + jnp.einsum("bthk,bhkv->bthv", qc * jnp.exp(b), state)
        b_end = b[:, -1]
        new = jnp.exp(b_end)[..., None] * state + jnp.einsum(
            "bshk,bshv->bhkv", kc * jnp.exp(b_end[:, None] - b), vc)
        return new, o

    state, o = lax.scan(step, state0, (to_chunks(q, L), to_chunks(k, L), to_chunks(v, L), to_chunks(log_f, L)))
    return from_chunks(o), state


def hgrn2_out(o, g, w):
    n, t = g.shape[:2]
    o = rms_norm(o, w) * jax.nn.silu(g.reshape(n, t, HG_HEADS, HG_DV).astype(F32))
    return o.reshape(n, t, HG_WIDTH).astype(g.dtype)


def gqa_attend(q, k, v):
    n, tq = q.shape[:2]
    qg = q.reshape(n, tq, AT_KV_HEADS, AT_GROUP, AT_HEAD_DIM)
    s = jnp.einsum("bqkgd,bskd->bkgqs", qg, k) * (AT_HEAD_DIM ** -0.5)
    p = jax.nn.softmax(s.astype(F32), axis=-1).astype(v.dtype)
    o = jnp.einsum("bkgqs,bskd->bqkgd", p, v)
    return o.reshape(n, tq, AT_WIDTH)


def dwconv_centred(u, w):
    return lax.conv_general_dilated(
        u, w[:, None, :].astype(u.dtype), window_strides=(1,),
        padding=[(MB_CONV // 2, MB_CONV // 2)], dimension_numbers=("NWC", "WIO", "NWC"),
        feature_group_count=u.shape[-1])


def mamba_inputs(xbc, dt_raw, conv_w, conv_b, dt_bias):
    n, t = xbc.shape[:2]
    u = jax.nn.silu(dwconv_centred(xbc, conv_w) + conv_b)
    xs, bm, cm = jnp.split(u, [MB_INNER, MB_INNER + MB_BC_WIDTH], axis=-1)
    xs = xs.reshape(n, t, MB_HEADS, MB_HEAD_DIM).astype(F32)
    bm = bm.reshape(n, t, MB_GROUPS, MB_STATE).astype(F32)
    cm = cm.reshape(n, t, MB_GROUPS, MB_STATE).astype(F32)
    dt = jax.nn.softplus(dt_raw.reshape(n, t, 2, MB_HEADS).astype(F32) + dt_bias.astype(F32))
    return xs, bm, cm, dt[:, :, 0], dt[:, :, 1]


def ssd_scan(xs, dt, a, bm, cm, state0):
    L = MB_CHUNK
    G, HPG = MB_GROUPS, MB_HEADS_PER_GROUP
    causal = jnp.tril(jnp.ones((L, L), dtype=bool))[None, :, :, None]

    def step(state, blk):
        xc, dtc, bc, cc = blk
        n = xc.shape[0]
        acum = jnp.cumsum(dtc * a, axis=1)
        seg = acum[:, :, None, :] - acum[:, None, :, :]
        decay = jnp.exp(jnp.where(causal, seg, -jnp.inf)).reshape(n, L, L, G, HPG)
        xdt = (xc * dtc[..., None]).reshape(n, L, G, HPG, MB_HEAD_DIM)
        cb = jnp.einsum("btgn,bsgn->btsg", cc, bc)
        y = jnp.einsum("btsg,btsgh,bsghp->btghp", cb, decay, xdt)
        sg = state.reshape(n, G, HPG, MB_HEAD_DIM, MB_STATE)
        y = y + jnp.einsum("btgn,bghpn,btgh->btghp", cc, sg, jnp.exp(acum).reshape(n, L, G, HPG))
        a_end = acum[:, -1]
        w_end = jnp.exp(a_end[:, None] - acum).reshape(n, L, G, HPG)
        new = jnp.exp(a_end)[..., None, None] * state + jnp.einsum(
            "bsgn,bsgh,bsghp->bghpn", bc, w_end, xdt).reshape(n, MB_HEADS, MB_HEAD_DIM, MB_STATE)
        return new, y.reshape(n, L, MB_HEADS, MB_HEAD_DIM)

    state, y = lax.scan(step, state0, (to_chunks(xs, L), to_chunks(dt, L), to_chunks(bm, L), to_chunks(cm, L)))
    return from_chunks(y), state


def mamba_out(y, xs, z, d, w):
    n, t = z.shape[:2]
    y = (y + d.astype(F32)[:, None] * xs).reshape(n, t, MB_INNER) * jax.nn.silu(z.astype(F32))
    yg = y.reshape(n, t, MB_GROUPS, -1)
    yg = yg * lax.rsqrt(jnp.mean(yg * yg, axis=-1, keepdims=True) + RMS_EPS)
    return (yg.reshape(n, t, MB_INNER) * w).astype(z.dtype)


def mixer_sublayer(h_lat, h_ctx, w_in, lb_f, lb_b, hg_gnorm, at_qnorm, at_knorm,
                   mb_conv_w, mb_conv_b, mb_dt_bias, mb_a_log, mb_d, mb_norm,
                   w_br_hg, w_br_at, w_br_mb, w_out, rope_cos, rope_sin, with_ctx):
    n, t = h_lat.shape[:2]
    tc = h_ctx.shape[1]
    (hq_l, hff_l, hfb_l, hi_l, hg_l, aq_l, ak_l, av_l, mz_l, mx_l, mdt_l, gt_l) = jnp.split(h_lat @ w_in, IN_SPLITS, axis=-1)
    (hq_c, hff_c, hfb_c, hi_c, hg_c, aq_c, ak_c, av_c, mz_c, mx_c, mdt_c, gt_c) = jnp.split(h_ctx @ w_in, IN_SPLITS, axis=-1)

    q_l, i_l, lf_lf, k_lf, lf_lb, k_lb = hgrn2_inputs(hq_l, hff_l, hfb_l, hi_l, lb_f, lb_b)
    q_c, i_c, lf_cf, k_cf, lf_cb, k_cb = hgrn2_inputs(hq_c, hff_c, hfb_c, hi_c, lb_f, lb_b)
    s0 = jnp.zeros((n, HG_HEADS, HG_DK, HG_DV), F32)
    oc_f, s_f = hgrn2_scan(q_c, k_cf, i_c, lf_cf, s0)
    oc_b, s_b = hgrn2_scan(flip(q_c), flip(k_cb), flip(i_c), flip(lf_cb), s0)
    ol_f, _ = hgrn2_scan(q_l, k_lf, i_l, lf_lf, s_f)
    ol_b, _ = hgrn2_scan(flip(q_l), flip(k_lb), flip(i_l), flip(lf_lb), s_b)
    o_hg_l = hgrn2_out(ol_f + flip(ol_b), hg_l, hg_gnorm)

    heads = lambda a, h: a.reshape(a.shape[0], a.shape[1], h, AT_HEAD_DIM)
    q_l = apply_rope(rms_norm(heads(aq_l, AT_HEADS), at_qnorm), rope_cos, rope_sin)
    k_l = apply_rope(rms_norm(heads(ak_l, AT_KV_HEADS), at_knorm), rope_cos, rope_sin)
    k_c = rms_norm(heads(ak_c, AT_KV_HEADS), at_knorm)
    v_c = heads(av_c, AT_KV_HEADS)
    k_all = jnp.concatenate([k_c, k_l], axis=1)
    v_all = jnp.concatenate([v_c, heads(av_l, AT_KV_HEADS)], axis=1)
    o_at_l = from_chunks(lax.map(lambda qb: gqa_attend(qb, k_all, v_all), to_chunks(q_l, AT_BLOCK)))

    a = -jnp.exp(mb_a_log.astype(F32))
    xs_c, bm_c, cm_c, dtf_c, dtb_c = mamba_inputs(mx_c, mdt_c, mb_conv_w, mb_conv_b, mb_dt_bias)
    xs_l, bm_l, cm_l, dtf_l, dtb_l = mamba_inputs(mx_l, mdt_l, mb_conv_w, mb_conv_b, mb_dt_bias)
    z0 = jnp.zeros((n, MB_HEADS, MB_HEAD_DIM, MB_STATE), F32)
    yc_f, st_f = ssd_scan(xs_c, dtf_c, a[0], bm_c, cm_c, z0)
    yc_b, st_b = ssd_scan(flip(xs_c), flip(dtb_c), a[1], flip(bm_c), flip(cm_c), z0)
    yl_f, _ = ssd_scan(xs_l, dtf_l, a[0], bm_l, cm_l, st_f)
    yl_b, _ = ssd_scan(flip(xs_l), flip(dtb_l), a[1], flip(bm_l), flip(cm_l), st_b)
    o_mb_l = mamba_out(yl_f + flip(yl_b), xs_l, mz_l, mb_d, mb_norm)

    def merge(o_hg, o_at, o_mb, gates):
        g_hg, g_at, g_mb = jnp.split(jax.nn.sigmoid(gates), 3, axis=-1)
        y = g_hg * (o_hg @ w_br_hg) + g_at * (o_at @ w_br_at) + g_mb * (o_mb @ w_br_mb)
        return y @ w_out

    y_lat = merge(o_hg_l, o_at_l, o_mb_l, gt_l)
    if not with_ctx:
        return y_lat, None
    o_hg_c = hgrn2_out(oc_f + flip(oc_b), hg_c, hg_gnorm)
    o_at_c = gqa_attend(rms_norm(heads(aq_c, AT_HEADS), at_qnorm), k_c, v_c)
    o_mb_c = mamba_out(yc_f + flip(yc_b), xs_c, mz_c, mb_d, mb_norm)
    y_ctx = merge(o_hg_c, o_at_c, o_mb_c, gt_c)
    return y_lat, y_ctx


def swiglu_ffn(h, w_in, w_out):
    gate, up = jnp.split(h @ w_in, 2, axis=-1)
    return (jax.nn.silu(gate) * up) @ w_out


def setup_inputs(seed: int = 0) -> dict:
    key = jax.random.key(seed)
    ks = jax.random.split(key, 28)
    nrm = lambda k, shape, scale: jax.random.normal(k, shape, F32) * scale
    gain = lambda k, shape: 1.0 + 0.02 * jax.random.normal(k, shape, F32)
    dt0 = jnp.exp(jax.random.uniform(ks[13], (DEPTH, 2, MB_HEADS), F32, math.log(1e-3), math.log(1e-1)))
    return {
        "x": nrm(ks[0], (BATCH, SEQ, D_MODEL), 1.0),
        "c": nrm(ks[1], (BATCH, D_MODEL), 1.0),
        "ctx": nrm(ks[2], (BATCH, CTX_LEN, D_MODEL), 1.0),
        "c_ctx": nrm(ks[3], (D_MODEL,), 1.0),
        "w_mod": nrm(ks[4], (DEPTH, D_MODEL, 6 * D_MODEL), 0.5 * D_MODEL ** -0.5),
        "b_mod": nrm(ks[5], (DEPTH, 6 * D_MODEL), 0.02),
        "w_in": nrm(ks[6], (DEPTH, D_MODEL, IN_WIDTH), D_MODEL ** -0.5),
        "hg_lb": nrm(ks[7], (DEPTH, 2, HG_WIDTH), 0.1),
        "hg_gnorm": gain(ks[8], (DEPTH, HG_DV)),
        "at_qnorm": gain(ks[9], (DEPTH, AT_HEAD_DIM)),
        "at_knorm": gain(ks[10], (DEPTH, AT_HEAD_DIM)),
        "mb_conv_w": nrm(ks[11], (DEPTH, MB_CONV, MB_CONV_DIM), MB_CONV ** -0.5),
        "mb_conv_b": nrm(ks[12], (DEPTH, MB_CONV_DIM), 0.02),
        "mb_dt_bias": dt0 + jnp.log(-jnp.expm1(-dt0)),
        "mb_a_log": jnp.log(jax.random.uniform(ks[14], (DEPTH, 2, MB_HEADS), F32, 1.0, 16.0)),
        "mb_d": gain(ks[15], (DEPTH, MB_HEADS)),
        "mb_norm": gain(ks[16], (DEPTH, MB_INNER)),
        "w_br_hg": nrm(ks[17], (DEPTH, HG_WIDTH, D_MODEL), HG_WIDTH ** -0.5),
        "w_br_at": nrm(ks[18], (DEPTH, AT_WIDTH, D_MODEL), AT_WIDTH ** -0.5),
        "w_br_mb": nrm(ks[19], (DEPTH, MB_INNER, D_MODEL), MB_INNER ** -0.5),
        "w_out": nrm(ks[20], (DEPTH, D_MODEL, D_MODEL), DN_BETA * D_MODEL ** -0.5),
        "ln1_g": gain(ks[21], (DEPTH, D_MODEL)),
        "ln1_b": nrm(ks[22], (DEPTH, D_MODEL), 0.02),
        "w_ffn_in": nrm(ks[23], (DEPTH, D_MODEL, 2 * FFN_HIDDEN), D_MODEL ** -0.5),
        "w_ffn_out": nrm(ks[24], (DEPTH, FFN_HIDDEN, D_MODEL), DN_BETA * FFN_HIDDEN ** -0.5),
        "ln2_g": gain(ks[25], (DEPTH, D_MODEL)),
        "ln2_b": nrm(ks[26], (DEPTH, D_MODEL), 0.02),
    }


def reference(x, c, ctx, c_ctx, w_mod, b_mod, w_in, hg_lb, hg_gnorm, at_qnorm, at_knorm,
              mb_conv_w, mb_conv_b, mb_dt_bias, mb_a_log, mb_d, mb_norm,
              w_br_hg, w_br_at, w_br_mb, w_out, ln1_g, ln1_b, w_ffn_in, w_ffn_out, ln2_g, ln2_b):
    rope_cos, rope_sin = axial_rope(x.shape[1])
    lbs = jnp.cumsum(jax.nn.softmax(hg_lb.astype(F32), axis=0), axis=0)
    lbs = lbs - lbs[0]
    for l in range(DEPTH):
        with_ctx = l < DEPTH - 1
        mod_l = (jax.nn.silu(c) @ w_mod[l] + b_mod[l])[:, None, :]
        mod_c = jax.nn.silu(c_ctx) @ w_mod[l] + b_mod[l]
        sh1, sc1, g1, sh2, sc2, g2 = jnp.split(mod_l, 6, axis=-1)
        csh1, csc1, cg1, csh2, csc2, cg2 = jnp.split(mod_c, 6, axis=-1)
        y_l, y_c = mixer_sublayer(
            x * (1.0 + sc1) + sh1, ctx * (1.0 + csc1) + csh1, w_in[l],
            lbs[l, 0].reshape(HG_HEADS, HG_DK), lbs[l, 1].reshape(HG_HEADS, HG_DK), hg_gnorm[l],
            at_qnorm[l], at_knorm[l], mb_conv_w[l], mb_conv_b[l], mb_dt_bias[l], mb_a_log[l],
            mb_d[l], mb_norm[l], w_br_hg[l], w_br_at[l], w_br_mb[l], w_out[l],
            rope_cos, rope_sin, with_ctx)
        x = layer_norm(DN_ALPHA * x + g1 * y_l, ln1_g[l], ln1_b[l])
        x = layer_norm(DN_ALPHA * x + g2 * swiglu_ffn(x * (1.0 + sc2) + sh2, w_ffn_in[l], w_ffn_out[l]),
                       ln2_g[l], ln2_b[l])
        if with_ctx:
            ctx = layer_norm(DN_ALPHA * ctx + cg1 * y_c, ln1_g[l], ln1_b[l])
            ctx = layer_norm(DN_ALPHA * ctx + cg2 * swiglu_ffn(ctx * (1.0 + csc2) + csh2, w_ffn_in[l], w_ffn_out[l]),
                             ln2_g[l], ln2_b[l])
    return x
```

```python
import functools
import math

import jax
import jax.numpy as jnp
from jax import lax
from jax.experimental import pallas as pl
from jax.experimental.pallas import tpu as pltpu

F32 = jnp.float32
BF16 = jnp.bfloat16

D_MODEL = 1024
HG_HEADS = 8
HG_DK = 128
HG_DV = 128
HG_WIDTH = HG_HEADS * HG_DK
AT_HEADS = 8
AT_KV_HEADS = 2
AT_GROUP = AT_HEADS // AT_KV_HEADS
AT_HEAD_DIM = 128
AT_WIDTH = AT_HEADS * AT_HEAD_DIM
AT_KV_WIDTH = AT_KV_HEADS * AT_HEAD_DIM
GRID_W = 64
ROPE_THETA = 10000.0
MB_INNER = 2 * D_MODEL
MB_HEAD_DIM = 64
MB_HEADS = MB_INNER // MB_HEAD_DIM
MB_STATE = 128
MB_GROUPS = 4
MB_HEADS_PER_GROUP = MB_HEADS // MB_GROUPS
MB_CONV = 5
MB_BC_WIDTH = MB_GROUPS * MB_STATE
MB_CONV_DIM = MB_INNER + 2 * MB_BC_WIDTH
MB_GROUP_WIDTH = MB_INNER // MB_GROUPS
FFN_HIDDEN = ((8 * D_MODEL + 3 * 256 - 1) // (3 * 256)) * 256
IN_SIZES = (HG_WIDTH,) * 5 + (AT_WIDTH, AT_KV_WIDTH, AT_KV_WIDTH) + (
    MB_INNER, MB_CONV_DIM, 2 * MB_HEADS) + (3 * D_MODEL,)
LN_EPS = 1e-5
RMS_EPS = 1e-6

LANES = 128
SUBLANES = 8
VMEM_LIMIT_BYTES = 56 * 1024 * 1024

ROW_TILE = 256
HG_CHUNK = 32
MB_CHUNK = 128
MODS_ROWS = 24
XDT_WIDTH = MB_CONV_DIM + LANES
CONV_CH_TILE = 512


def _cparams(*sem):
    return pltpu.CompilerParams(dimension_semantics=sem, vmem_limit_bytes=VMEM_LIMIT_BYTES)


def _resident(shape):
    nd = len(shape)
    return pl.BlockSpec(shape, lambda *_: (0,) * nd, pipeline_mode=pl.Buffered(1))


def _sigmoid(x):
    return 1.0 / (1.0 + jnp.exp(-x))


def _silu(x):
    return x * _sigmoid(x)


def _split_bf16(x):
    hi = x.astype(BF16)
    lo = (x - hi.astype(F32)).astype(BF16)
    return hi, lo


def _dot(a, b):
    return jnp.dot(a, b, preferred_element_type=F32)


def _dot_nt(a, b):
    return lax.dot_general(a, b, (((1,), (1,)), ((), ())), preferred_element_type=F32)


def _dot_tn(a, b):
    return lax.dot_general(a, b, (((0,), (0,)), ((), ())), preferred_element_type=F32)


def _dot2(m_bf16, x):
    hi, lo = _split_bf16(x)
    return _dot(m_bf16, hi) + _dot(m_bf16, lo)


def _dot2_right(x, m_bf16):
    hi, lo = _split_bf16(x)
    return _dot(hi, m_bf16) + _dot(lo, m_bf16)


def _layer_norm(v, g, b):
    mu = jnp.mean(v, axis=-1, keepdims=True)
    c = v - mu
    var = jnp.mean(c * c, axis=-1, keepdims=True)
    return c * lax.rsqrt(var + LN_EPS) * g + b


def _lbs_kernel(lb_ref, o_ref, *, depth):
    x = lb_ref[...]
    m = jnp.max(x, axis=0, keepdims=True)
    e = jnp.exp(x - m)
    p = e / jnp.sum(e, axis=0, keepdims=True)
    acc = p[0:1]
    for l in range(depth):
        if l > 0:
            acc = acc + p[l:l + 1]
        o_ref[l:l + 1, :] = acc - p[0:1]


def _lower_bounds(hg_lb):
    depth = hg_lb.shape[0]
    flat = hg_lb.reshape(depth, 2 * HG_WIDTH)
    out = pl.pallas_call(
        functools.partial(_lbs_kernel, depth=depth),
        out_shape=jax.ShapeDtypeStruct(flat.shape, F32),
        name="hg_lower_bounds",
    )(flat)
    return out.reshape(depth, 2, 1, HG_WIDTH)


def _mods_kernel(c_ref, w_ref, b_ref, o_ref):
    s = _silu(c_ref[...])
    o_ref[0] = _dot(s.astype(BF16), w_ref[0]) + b_ref[0]


def _modulations(cond, w_mod, b_mod):
    depth = w_mod.shape[0]
    nblk = w_mod.shape[2] // D_MODEL
    return pl.pallas_call(
        _mods_kernel,
        grid=(depth, nblk),
        in_specs=[
            pl.BlockSpec((MODS_ROWS, D_MODEL), lambda l, j: (0, 0)),
            pl.BlockSpec((1, D_MODEL, D_MODEL), lambda l, j: (l, 0, j)),
            pl.BlockSpec((1, 1, D_MODEL), lambda l, j: (l, 0, j)),
        ],
        out_specs=pl.BlockSpec((1, MODS_ROWS, D_MODEL), lambda l, j: (l, 0, j)),
        out_shape=jax.ShapeDtypeStruct((depth, MODS_ROWS, nblk * D_MODEL), F32),
        compiler_params=_cparams("arbitrary", "arbitrary"),
        name="adaln_modulations",
    )(cond, w_mod, b_mod.reshape(depth, 1, -1))


def _mod_spec(chunk, nct):
    return pl.BlockSpec((1, 1, D_MODEL),
                        lambda b, i: (2 * b + jnp.where(i >= nct, 1, 0), 0, chunk))


def _inproj_kernel(x_ref, sc_ref, sh_ref, w_ref, o_ref):
    h = x_ref[0] * (1.0 + sc_ref[0]) + sh_ref[0]
    o_ref[0] = _dot(h.astype(BF16), w_ref[...]).astype(o_ref.dtype)


def _inproj(xs, modsel, w, nct, name):
    bsz, s, _ = xs.shape
    n = w.shape[1]
    return pl.pallas_call(
        _inproj_kernel,
        grid=(bsz, s // ROW_TILE),
        in_specs=[
            pl.BlockSpec((1, ROW_TILE, D_MODEL), lambda b, i: (b, i, 0)),
            _mod_spec(1, nct), _mod_spec(0, nct),
            _resident(w.shape),
        ],
        out_specs=pl.BlockSpec((1, ROW_TILE, n), lambda b, i: (b, i, 0)),
        out_shape=jax.ShapeDtypeStruct((bsz, s, n), F32),
        compiler_params=_cparams("parallel", "arbitrary"),
        name=name,
    )(xs, modsel, modsel, w)


def _inproj_attn_kernel(x_ref, sc_ref, sh_ref, w_ref, cos_ref, sin_ref, qn_ref, kn_ref,
                        q_ref, k_ref, v_ref):
    h = x_ref[0] * (1.0 + sc_ref[0]) + sh_ref[0]
    p = _dot(h.astype(BF16), w_ref[...])
    cos = cos_ref[...]
    sin = sin_ref[...]
    lane = lax.broadcasted_iota(jnp.int32, cos.shape, 1)
    even = (lane % 2) == 0

    def norm_rope(xh, w):
        ms = jnp.mean(xh * xh, axis=-1, keepdims=True)
        xn = xh * lax.rsqrt(ms + RMS_EPS) * w
        partner = jnp.where(even, pltpu.roll(xn, AT_HEAD_DIM - 1, 1), pltpu.roll(xn, 1, 1))
        return xn * cos + partner * sin

    scale = AT_HEAD_DIM ** -0.5
    for hh in range(AT_HEADS):
        sl = slice(hh * AT_HEAD_DIM, (hh + 1) * AT_HEAD_DIM)
        q_ref[0, :, sl] = (norm_rope(p[:, sl], qn_ref[...]) * scale).astype(BF16)
    for kv in range(AT_KV_HEADS):
        sl = slice(kv * AT_HEAD_DIM, (kv + 1) * AT_HEAD_DIM)
        src = slice(AT_WIDTH + kv * AT_HEAD_DIM, AT_WIDTH + (kv + 1) * AT_HEAD_DIM)
        k_ref[0, :, sl] = norm_rope(p[:, src], kn_ref[...]).astype(BF16)
    v_ref[0] = p[:, AT_WIDTH + AT_KV_WIDTH:].astype(BF16)


def _inproj_attn(xs, modsel, w, cos, sin, qn, kn, nct):
    bsz, s, _ = xs.shape
    row = lambda width: pl.BlockSpec((1, ROW_TILE, width), lambda b, i: (b, i, 0))
    return pl.pallas_call(
        _inproj_attn_kernel,
        grid=(bsz, s // ROW_TILE),
        in_specs=[
            row(D_MODEL), _mod_spec(1, nct), _mod_spec(0, nct),
            _resident(w.shape),
            pl.BlockSpec((ROW_TILE, AT_HEAD_DIM), lambda b, i: (i, 0)),
            pl.BlockSpec((ROW_TILE, AT_HEAD_DIM), lambda b, i: (i, 0)),
            _resident((1, AT_HEAD_DIM)), _resident((1, AT_HEAD_DIM)),
        ],
        out_specs=[row(AT_WIDTH), row(AT_KV_WIDTH), row(AT_KV_WIDTH)],
        out_shape=[jax.ShapeDtypeStruct((bsz, s, AT_WIDTH), BF16),
                   jax.ShapeDtypeStruct((bsz, s, AT_KV_WIDTH), BF16),
                   jax.ShapeDtypeStruct((bsz, s, AT_KV_WIDTH), BF16)],
        compiler_params=_cparams("parallel", "arbitrary"),
        name="inproj_attn",
    )(xs, modsel, modsel, w, cos, sin, qn.reshape(1, -1), kn.reshape(1, -1))


def _attn_kernel(q_ref, k_ref, v_ref, o_ref, *, nct, ctx_len):
    i = pl.program_id(2)

    def attend(keys, vals):
        for hh in range(AT_GROUP):
            sl = slice(hh * AT_HEAD_DIM, (hh + 1) * AT_HEAD_DIM)
            s = _dot_nt(q_ref[0, :, sl], keys)
            m = jnp.max(s, axis=-1, keepdims=True)
            e = jnp.exp(s - m)
            l = jnp.sum(e, axis=-1, keepdims=True)
            o = _dot(e.astype(BF16), vals) / l
            o_ref[0, :, sl] = o.astype(o_ref.dtype)

    @pl.when(i < nct)
    def _():
        attend(k_ref[0, :ctx_len, :], v_ref[0, :ctx_len, :])

    @pl.when(i >= nct)
    def _():
        attend(k_ref[0], v_ref[0])


def _attention(q, k, v, nct, ctx_len):
    bsz, s, _ = q.shape
    gw = AT_GROUP * AT_HEAD_DIM
    kv_spec = pl.BlockSpec((1, s, AT_HEAD_DIM), lambda b, g, i: (b, 0, g))
    return pl.pallas_call(
        functools.partial(_attn_kernel, nct=nct, ctx_len=ctx_len),
        grid=(bsz, AT_KV_HEADS, s // ROW_TILE),
        in_specs=[pl.BlockSpec((1, ROW_TILE, gw), lambda b, g, i: (b, i, g)), kv_spec, kv_spec],
        out_specs=pl.BlockSpec((1, ROW_TILE, gw), lambda b, g, i: (b, i, g)),
        out_shape=jax.ShapeDtypeStruct((bsz, s, AT_WIDTH), BF16),
        compiler_params=_cparams("parallel", "parallel", "arbitrary"),
        name="gqa_attention",
    )(q, k, v)


def _visit_to_tile(p, v, nct, nt):
    backward = jnp.where(v < nct, nct - 1 - v, nt - 1 - (v - nct))
    return jnp.where(p == 1, v, backward)


def _hgrn2_kernel(q_ref, f_ref, i_ref, g_ref, lb_ref, gn_ref, tri_ref, o_ref,
                  qs_s, kk_s, b_s, v_s, st_s, acc_s, *, nct, nt):
    p = pl.program_id(1)
    v = pl.program_id(2)
    row0 = pl.multiple_of(_visit_to_tile(p, v, nct, nt) * ROW_TILE, ROW_TILE)
    fwd = p == 1
    n_chunks = ROW_TILE // HG_CHUNK

    @pl.when(v == 0)
    def _():
        st_s[...] = jnp.zeros_like(st_s)

    xq = q_ref[0]
    qs_s[...] = _silu(xq) * (HG_DK ** -0.5)
    lb = lb_ref[0]
    xf = f_ref[0]
    e = jnp.exp(-jnp.abs(xf))
    r = 1.0 / (1.0 + e)
    sig_pos = jnp.where(xf >= 0, r, e * r)
    sig_neg = jnp.where(xf >= 0, e * r, r)
    log_f = jnp.log(lb + (1.0 - lb) * sig_pos)
    kk_s[...] = (1.0 - lb) * sig_neg
    b_s[...] = _dot2(tri_ref[0], log_f)
    v_s[...] = i_ref[0].astype(BF16)

    trow = lax.broadcasted_iota(jnp.int32, (HG_CHUNK, HG_CHUNK), 0)
    tcol = lax.broadcasted_iota(jnp.int32, (HG_CHUNK, HG_CHUNK), 1)
    causal = jnp.where(fwd, tcol - trow, trow - tcol) <= 0
    mid_off = jnp.where(fwd, HG_CHUNK // 2 - 1, HG_CHUNK // 2)
    end_off = jnp.where(fwd, HG_CHUNK - 1, 0)

    def chunk_body(ci, carry):
        c = jnp.where(fwd, ci, n_chunks - 1 - ci)
        r0 = pl.multiple_of(c * HG_CHUNK, HG_CHUNK)
        rows = pl.ds(r0, HG_CHUNK)
        b = b_s[rows, :]
        b_mid = b_s[pl.ds(r0 + mid_off, 1), :]
        b_end = b_s[pl.ds(r0 + end_off, 1), :]
        qq = qs_s[rows, :]
        kk = kk_s[rows, :]
        q_in = (qq * jnp.exp(b - b_mid)).astype(BF16)
        k_in = (kk * jnp.exp(b_mid - b)).astype(BF16)
        q_st = (qq * jnp.exp(b)).astype(BF16)
        k_st = (kk * jnp.exp(b_end - b)).astype(BF16)
        decay = jnp.exp(b_end)
        vv = v_s[rows, :]
        acc_rows = pl.ds(row0 + r0, HG_CHUNK)
        for h in range(HG_HEADS):
            sl = slice(h * HG_DK, (h + 1) * HG_DK)
            a = jnp.where(causal, _dot_nt(q_in[:, sl], k_in[:, sl]), 0.0).astype(BF16)
            st = st_s[h]
            o = _dot(a, vv[:, sl]) + _dot_nt(q_st[:, sl], st.astype(BF16))
            st_s[h] = st * decay[:, sl] + _dot_tn(vv[:, sl], k_st[:, sl])

            @pl.when(p == 0)
            def _():
                acc_s[acc_rows, sl] = o

            @pl.when(p == 1)
            def _():
                acc_s[acc_rows, sl] = acc_s[acc_rows, sl] + o
        return carry

    lax.fori_loop(0, n_chunks, chunk_body, 0)

    @pl.when(p == 1)
    def _():
        gate = _silu(g_ref[0])
        for h in range(HG_HEADS):
            sl = slice(h * HG_DV, (h + 1) * HG_DV)
            o = acc_s[pl.ds(row0, ROW_TILE), sl]
            ms = jnp.mean(o * o, axis=-1, keepdims=True)
            o_ref[0, :, sl] = (o * lax.rsqrt(ms + RMS_EPS) * gn_ref[...] * gate[:, sl]).astype(o_ref.dtype)


def _hgrn2(p_hg, lb, gnorm, tri, nct):
    bsz, s, _ = p_hg.shape
    nt = s // ROW_TILE
    tile = lambda p, v: _visit_to_tile(p, v, nct, nt)
    col = lambda c: pl.BlockSpec((1, ROW_TILE, HG_WIDTH), lambda b, p, v: (b, tile(p, v), c))
    return pl.pallas_call(
        functools.partial(_hgrn2_kernel, nct=nct, nt=nt),
        grid=(bsz, 2, nt),
        in_specs=[
            col(0),
            pl.BlockSpec((1, ROW_TILE, HG_WIDTH), lambda b, p, v: (b, tile(p, v), 2 - p)),
            col(3),
            pl.BlockSpec((1, ROW_TILE, HG_WIDTH), lambda b, p, v: (b, tile(p, v) * p, 4)),
            pl.BlockSpec((1, 1, HG_WIDTH), lambda b, p, v: (1 - p, 0, 0)),
            pl.BlockSpec((1, HG_DV), lambda b, p, v: (0, 0)),
            pl.BlockSpec((1, ROW_TILE, ROW_TILE), lambda b, p, v: (p, 0, 0)),
        ],
        out_specs=pl.BlockSpec((1, ROW_TILE, HG_WIDTH), lambda b, p, v: (b, tile(p, v) * p, 0)),
        out_shape=jax.ShapeDtypeStruct((bsz, s, HG_WIDTH), BF16),
        scratch_shapes=[
            pltpu.VMEM((ROW_TILE, HG_WIDTH), F32),
            pltpu.VMEM((ROW_TILE, HG_WIDTH), F32),
            pltpu.VMEM((ROW_TILE, HG_WIDTH), F32),
            pltpu.VMEM((ROW_TILE, HG_WIDTH), BF16),
            pltpu.VMEM((HG_HEADS, HG_DV, HG_DK), F32),
            pltpu.VMEM((s, HG_WIDTH), F32),
        ],
        compiler_params=_cparams("parallel", "arbitrary", "arbitrary"),
        name="hgrn2_bidir",
    )(p_hg, p_hg, p_hg, p_hg, lb, gnorm.reshape(1, -1), tri)


def _conv_kernel(x_ref, w_ref, b_ref, o_ref, pad_s, *, regions):
    half = MB_CONV // 2
    w = w_ref[...]
    bias = b_ref[...]
    zeros = jnp.zeros((SUBLANES, x_ref.shape[2]), F32)
    for r0, n in regions:
        pad_s[0:SUBLANES, :] = zeros
        pad_s[SUBLANES:SUBLANES + n, :] = x_ref[0, r0:r0 + n, :]
        pad_s[SUBLANES + n:2 * SUBLANES + n, :] = zeros
        acc = bias + w[0:1] * pad_s[SUBLANES - half:SUBLANES - half + n, :]
        for j in range(1, MB_CONV):
            acc = acc + w[j:j + 1] * pad_s[SUBLANES - half + j:SUBLANES - half + j + n, :]
        o_ref[0, r0:r0 + n, :] = _silu(acc)


def _conv_silu(p_xdt, conv_w, conv_b, ctx_len):
    bsz, s, _ = p_xdt.shape
    regions = ((0, ctx_len), (ctx_len, s - ctx_len))
    blk = pl.BlockSpec((1, s, CONV_CH_TILE), lambda b, j: (b, 0, j))
    return pl.pallas_call(
        functools.partial(_conv_kernel, regions=regions),
        grid=(bsz, MB_CONV_DIM // CONV_CH_TILE),
        in_specs=[blk,
                  pl.BlockSpec((MB_CONV, CONV_CH_TILE), lambda b, j: (0, j)),
                  pl.BlockSpec((1, CONV_CH_TILE), lambda b, j: (0, j))],
        out_specs=blk,
        out_shape=jax.ShapeDtypeStruct((bsz, s, MB_CONV_DIM), F32),
        scratch_shapes=[pltpu.VMEM((s + 2 * SUBLANES, CONV_CH_TILE), F32)],
        compiler_params=_cparams("parallel", "arbitrary"),
        name="ssd_conv_silu",
    )(p_xdt, conv_w, conv_b.reshape(1, -1))


def _ssd_kernel(u_ref, dt_ref, z_ref, dtb_ref, alog_ref, dexp_ref, nw_ref, tril_ref, triu_ref,
                exp_ref, o_ref, st_s, acc_s, *, nct, nt):
    p = pl.program_id(1)
    v = pl.program_id(2)
    row0 = pl.multiple_of(_visit_to_tile(p, v, nct, nt) * MB_CHUNK, MB_CHUNK)
    fwd = p == 1
    L = MB_CHUNK

    @pl.when(v == 0)
    def _():
        st_s[...] = jnp.zeros_like(st_s)

    u = u_ref[0]
    xs = u[:, :MB_INNER]
    bm = u[:, MB_INNER:MB_INNER + MB_BC_WIDTH].astype(BF16)
    cm = u[:, MB_INNER + MB_BC_WIDTH:].astype(BF16)

    xdt = dt_ref[0] + dtb_ref[...]
    dt = jnp.maximum(xdt, 0.0) + jnp.log(1.0 + jnp.exp(-jnp.abs(xdt)))
    dta = dt * (-jnp.exp(alog_ref[...]))
    cum_f = _dot2(tril_ref[...], dta)
    cum_b = _dot2(triu_ref[...], dta)
    cum = jnp.where(fwd, cum_f, cum_b)
    cum_end = jnp.where(fwd, cum[L - 1:L, :], cum[0:1, :])
    expand = exp_ref[0]
    e_in = _dot2_right(jnp.exp(cum), expand)
    e_out = _dot2_right(jnp.exp(cum_end - cum) * dt, expand)
    e_end = jnp.where(fwd, e_in[L - 1:L, :], e_in[0:1, :])
    x_out = (xs * e_out).astype(BF16)

    y_parts = []
    for g in range(MB_GROUPS):
        gsl = slice(g * MB_GROUP_WIDTH, (g + 1) * MB_GROUP_WIDTH)
        nsl = slice(g * MB_STATE, (g + 1) * MB_STATE)
        st = st_s[:, gsl]
        y_parts.append(_dot(cm[:, nsl], st.astype(BF16)) * e_in[:, gsl])
        st_s[:, gsl] = st * e_end[:, gsl] + _dot_tn(bm[:, nsl], x_out[:, gsl])
    y_inter = jnp.concatenate(y_parts, axis=1)

    rows = pl.ds(row0, L)

    @pl.when(p == 0)
    def _():
        acc_s[rows, :] = y_inter

    @pl.when(p == 1)
    def _():
        trow = lax.broadcasted_iota(jnp.int32, (L, L), 0)
        tcol = lax.broadcasted_iota(jnp.int32, (L, L), 1)
        lower = tcol <= trow
        upper = tcol >= trow
        lane = lax.broadcasted_iota(jnp.int32, (L, LANES), 1)
        first_half = lane < MB_HEAD_DIM
        cum_f_t = cum_f.T
        cum_b_t = cum_b.T
        dt_t = dt.T
        xs16 = xs.astype(BF16)
        neg_inf = jnp.float32(-jnp.inf)
        pieces = []
        for g in range(MB_GROUPS):
            nsl = slice(g * MB_STATE, (g + 1) * MB_STATE)
            cb = _dot_nt(cm[:, nsl], bm[:, nsl])
            for hp in range(MB_HEADS_PER_GROUP // 2):
                ys = []
                for k in range(2):
                    h = g * MB_HEADS_PER_GROUP + 2 * hp + k
                    hb = MB_HEADS + h
                    seg_f = cum_f[:, h:h + 1] - cum_f_t[h:h + 1, :]
                    seg_b = cum_b[:, hb:hb + 1] - cum_b_t[hb:hb + 1, :]
                    m_f = jnp.exp(jnp.where(lower, seg_f, neg_inf)) * dt_t[h:h + 1, :]
                    m_b = jnp.exp(jnp.where(upper, seg_b, neg_inf)) * dt_t[hb:hb + 1, :]
                    m = (cb * (m_f + m_b)).astype(BF16)
                    c0 = (h // 2) * LANES
                    ys.append(_dot(m, xs16[:, c0:c0 + LANES]))
                pieces.append(jnp.where(first_half, ys[0], ys[1]))
        y = jnp.concatenate(pieces, axis=1) + y_inter + acc_s[rows, :]
        y = (y + dexp_ref[...] * xs) * _silu(z_ref[0])
        for g in range(MB_GROUPS):
            gsl = slice(g * MB_GROUP_WIDTH, (g + 1) * MB_GROUP_WIDTH)
            yg = y[:, gsl]
            ms = jnp.mean(yg * yg, axis=-1, keepdims=True)
            o_ref[0, :, gsl] = (yg * lax.rsqrt(ms + RMS_EPS) * nw_ref[:, gsl]).astype(o_ref.dtype)


def _ssd(u, p_xdt, z, dt_bias_row, a_log_row, d_exp, norm_w, tril, triu, expand, nct):
    bsz, s, _ = u.shape
    nt = s // MB_CHUNK
    tile = lambda p, v: _visit_to_tile(p, v, nct, nt)
    dt_block = MB_CONV_DIM // LANES
    return pl.pallas_call(
        functools.partial(_ssd_kernel, nct=nct, nt=nt),
        grid=(bsz, 2, nt),
        in_specs=[
            pl.BlockSpec((1, MB_CHUNK, MB_CONV_DIM), lambda b, p, v: (b, tile(p, v), 0)),
            pl.BlockSpec((1, MB_CHUNK, LANES), lambda b, p, v: (b, tile(p, v), dt_block)),
            pl.BlockSpec((1, MB_CHUNK, MB_INNER), lambda b, p, v: (b, tile(p, v) * p, 0)),
            pl.BlockSpec((1, LANES), lambda b, p, v: (0, 0)),
            pl.BlockSpec((1, LANES), lambda b, p, v: (0, 0)),
            pl.BlockSpec((1, MB_INNER), lambda b, p, v: (0, 0)),
            pl.BlockSpec((1, MB_INNER), lambda b, p, v: (0, 0)),
            pl.BlockSpec((MB_CHUNK, MB_CHUNK), lambda b, p, v: (0, 0)),
            pl.BlockSpec((MB_CHUNK, MB_CHUNK), lambda b, p, v: (0, 0)),
            pl.BlockSpec((1, LANES, MB_INNER), lambda b, p, v: (p, 0, 0)),
        ],
        out_specs=pl.BlockSpec((1, MB_CHUNK, MB_INNER), lambda b, p, v: (b, tile(p, v) * p, 0)),
        out_shape=jax.ShapeDtypeStruct((bsz, s, MB_INNER), BF16),
        scratch_shapes=[
            pltpu.VMEM((MB_STATE, MB_INNER), F32),
            pltpu.VMEM((s, MB_INNER), F32),
        ],
        compiler_params=_cparams("parallel", "arbitrary", "arbitrary"),
        name="ssd_bidir",
    )(u, p_xdt, z, dt_bias_row, a_log_row, d_exp, norm_w, tril, triu, expand)


def _merge_kernel(x_ref, ohg_ref, oat_ref, omb_ref, gt_ref, g1_ref, whg_ref, wat_ref, wmb_ref,
                  wout_ref, lng_ref, lnb_ref, o_ref, *, alpha):
    gates = _sigmoid(gt_ref[0])
    y = (gates[:, :D_MODEL] * _dot(ohg_ref[0], whg_ref[...])
         + gates[:, D_MODEL:2 * D_MODEL] * _dot(oat_ref[0], wat_ref[...])
         + gates[:, 2 * D_MODEL:] * _dot(omb_ref[0], wmb_ref[...]))
    y = _dot(y.astype(BF16), wout_ref[...])
    o_ref[0] = _layer_norm(alpha * x_ref[0] + g1_ref[0] * y, lng_ref[...], lnb_ref[...])


def _merge(xs, o_hg, o_at, o_mb, gates, modsel, w_hg, w_at, w_mb, w_out, ln_g, ln_b, nct, alpha):
    bsz, s, _ = xs.shape
    row = lambda width: pl.BlockSpec((1, ROW_TILE, width), lambda b, i: (b, i, 0))
    return pl.pallas_call(
        functools.partial(_merge_kernel, alpha=alpha),
        grid=(bsz, s // ROW_TILE),
        in_specs=[row(D_MODEL), row(HG_WIDTH), row(AT_WIDTH), row(MB_INNER), row(3 * D_MODEL),
                  _mod_spec(2, nct),
                  _resident(w_hg.shape), _resident(w_at.shape), _resident(w_mb.shape),
                  _resident(w_out.shape), _resident((1, D_MODEL)), _resident((1, D_MODEL))],
        out_specs=row(D_MODEL),
        out_shape=jax.ShapeDtypeStruct(xs.shape, F32),
        compiler_params=_cparams("parallel", "arbitrary"),
        name="merge_out_ln",
    )(xs, o_hg, o_at, o_mb, gates, modsel, w_hg, w_at, w_mb, w_out,
      ln_g.reshape(1, -1), ln_b.reshape(1, -1))


def _ffn_kernel(x_ref, sc_ref, sh_ref, g2_ref, win_ref, wout_ref, lng_ref, lnb_ref, o_ref, *, alpha):
    x = x_ref[0]
    h = x * (1.0 + sc_ref[0]) + sh_ref[0]
    gu = _dot(h.astype(BF16), win_ref[...])
    a = _silu(gu[:, :FFN_HIDDEN]) * gu[:, FFN_HIDDEN:]
    y = _dot(a.astype(BF16), wout_ref[...])
    o_ref[0] = _layer_norm(alpha * x + g2_ref[0] * y, lng_ref[...], lnb_ref[...])


def _ffn(xs, modsel, w_in, w_out, ln_g, ln_b, nct, alpha):
    bsz, s, _ = xs.shape
    row = pl.BlockSpec((1, ROW_TILE, D_MODEL), lambda b, i: (b, i, 0))
    return pl.pallas_call(
        functools.partial(_ffn_kernel, alpha=alpha),
        grid=(bsz, s // ROW_TILE),
        in_specs=[row, _mod_spec(4, nct), _mod_spec(3, nct), _mod_spec(5, nct),
                  _resident(w_in.shape), _resident(w_out.shape),
                  _resident((1, D_MODEL)), _resident((1, D_MODEL))],
        out_specs=row,
        out_shape=jax.ShapeDtypeStruct(xs.shape, F32),
        compiler_params=_cparams("parallel", "arbitrary"),
        name="swiglu_ffn_ln",
    )(xs, modsel, modsel, modsel, w_in, w_out, ln_g.reshape(1, -1), ln_b.reshape(1, -1))


def _rope_tables(seq, ctx_len):
    rows = seq // GRID_W
    row, col = jnp.meshgrid(jnp.arange(rows, dtype=F32), jnp.arange(GRID_W, dtype=F32), indexing="ij")
    n_pairs = AT_HEAD_DIM // 4
    inv_freq = ROPE_THETA ** (-jnp.arange(n_pairs, dtype=F32) / n_pairs)
    ang = jnp.concatenate([row.reshape(-1, 1) * inv_freq, col.reshape(-1, 1) * inv_freq], axis=-1)
    cos = jnp.repeat(jnp.cos(ang), 2, axis=-1)
    sin = jnp.repeat(jnp.sin(ang), 2, axis=-1) * jnp.tile(jnp.array([-1.0, 1.0], F32), AT_HEAD_DIM // 2)
    cos = jnp.concatenate([jnp.ones((ctx_len, AT_HEAD_DIM), F32), cos], axis=0)
    sin = jnp.concatenate([jnp.zeros((ctx_len, AT_HEAD_DIM), F32), sin], axis=0)
    return cos, sin


def _block_tri(n, blk):
    r = jnp.arange(n)
    same = (r[:, None] // blk) == (r[None, :] // blk)
    lower = (same & (r[None, :] <= r[:, None])).astype(BF16)
    upper = (same & (r[None, :] >= r[:, None])).astype(BF16)
    return lower, upper


def _head_expand():
    lane = jnp.arange(LANES)[:, None]
    head = jnp.arange(MB_INNER)[None, :] // MB_HEAD_DIM
    fwd = (lane == head).astype(BF16)
    bwd = (lane == head + MB_HEADS).astype(BF16)
    return jnp.stack([bwd, fwd])


def _pad_lanes(row):
    return jnp.pad(row.reshape(1, -1), ((0, 0), (0, LANES - row.size)))


def kernel(x, c, ctx, c_ctx, w_mod, b_mod, w_in, hg_lb, hg_gnorm, at_qnorm, at_knorm,
           mb_conv_w, mb_conv_b, mb_dt_bias, mb_a_log, mb_d, mb_norm,
           w_br_hg, w_br_at, w_br_mb, w_out, ln1_g, ln1_b, w_ffn_in, w_ffn_out, ln2_g, ln2_b):
    bsz, seq, _ = x.shape
    ctx_len = ctx.shape[1]
    depth = w_mod.shape[0]
    assert seq % ROW_TILE == 0 and ctx_len % ROW_TILE == 0 and seq % GRID_W == 0
    assert bsz + 1 <= MODS_ROWS
    alpha = (2 * depth) ** 0.25
    nct = ctx_len // ROW_TILE
    nct_mb = ctx_len // MB_CHUNK

    cos, sin = _rope_tables(seq, ctx_len)
    hg_tril, hg_triu = _block_tri(ROW_TILE, HG_CHUNK)
    hg_tri = jnp.stack([hg_triu, hg_tril])
    mb_tril, mb_triu = _block_tri(MB_CHUNK, MB_CHUNK)
    expand = _head_expand()

    lbs = _lower_bounds(hg_lb)
    cond = jnp.concatenate([c, c_ctx[None, :], jnp.zeros((MODS_ROWS - bsz - 1, D_MODEL), F32)], axis=0)
    mods = _modulations(cond, w_mod.astype(BF16), b_mod)
    ctx_rows = jnp.broadcast_to(mods[:, bsz:bsz + 1, :], (depth, bsz, mods.shape[-1]))
    modsel_all = jnp.stack([ctx_rows, mods[:, :bsz, :]], axis=2).reshape(depth, 2 * bsz, 1, -1)

    splits = [0]
    for width in IN_SIZES:
        splits.append(splits[-1] + width)
    o_hg, o_at, o_z, o_xbc, o_dt, o_gt = splits[0], splits[5], splits[8], splits[9], splits[10], splits[11]

    xs = jnp.concatenate([ctx, x], axis=1)
    for l in range(depth):
        modsel = modsel_all[l]
        w_l = w_in[l].astype(BF16)
        w_hg_in = w_l[:, o_hg:o_at]
        w_at_in = w_l[:, o_at:o_z]
        w_z_in = w_l[:, o_z:o_xbc]
        w_xdt_in = jnp.pad(w_l[:, o_xbc:o_gt], ((0, 0), (0, XDT_WIDTH - (o_gt - o_xbc))))
        w_gt_in = w_l[:, o_gt:]

        p_hg = _inproj(xs, modsel, w_hg_in, nct, "inproj_hgrn2")
        p_z = _inproj(xs, modsel, w_z_in, nct, "inproj_ssd_z")
        p_xdt = _inproj(xs, modsel, w_xdt_in, nct, "inproj_ssd_xdt")
        p_gt = _inproj(xs, modsel, w_gt_in, nct, "inproj_gates")
        q, k, v = _inproj_attn(xs, modsel, w_at_in, cos, sin, at_qnorm[l], at_knorm[l], nct)

        out_hg = _hgrn2(p_hg, lbs[l], hg_gnorm[l], hg_tri, nct)
        out_at = _attention(q, k, v, nct, ctx_len)
        u = _conv_silu(p_xdt, mb_conv_w[l], mb_conv_b[l], ctx_len)
        out_mb = _ssd(u, p_xdt, p_z, _pad_lanes(mb_dt_bias[l]), _pad_lanes(mb_a_log[l]),
                      jnp.repeat(mb_d[l], MB_HEAD_DIM).reshape(1, -1), mb_norm[l].reshape(1, -1),
                      mb_tril, mb_triu, expand, nct_mb)

        xs = _merge(xs, out_hg, out_at, out_mb, p_gt, modsel,
                    w_br_hg[l].astype(BF16), w_br_at[l].astype(BF16), w_br_mb[l].astype(BF16),
                    w_out[l].astype(BF16), ln1_g[l], ln1_b[l], nct, alpha)
        xs = _ffn(xs, modsel, w_ffn_in[l].astype(BF16), w_ffn_out[l].astype(BF16),
                  ln2_g[l], ln2_b[l], nct, alpha)
    return xs[:, ctx_len:, :]
```

```python
import functools
import math

import jax
import jax.numpy as jnp
from jax import lax
from jax.experimental import pallas as pl
from jax.experimental.pallas import tpu as pltpu

F32 = jnp.float32
BF16 = jnp.bfloat16

D_MODEL = 1024
HG_HEADS = 8
HG_DK = 128
HG_DV = 128
HG_WIDTH = HG_HEADS * HG_DK
AT_HEADS = 8
AT_KV_HEADS = 2
AT_GROUP = AT_HEADS // AT_KV_HEADS
AT_HEAD_DIM = 128
AT_WIDTH = AT_HEADS * AT_HEAD_DIM
AT_KV_WIDTH = AT_KV_HEADS * AT_HEAD_DIM
GRID_W = 64
ROPE_THETA = 10000.0
MB_INNER = 2 * D_MODEL
MB_HEAD_DIM = 64
MB_HEADS = MB_INNER // MB_HEAD_DIM
MB_STATE = 128
MB_GROUPS = 4
MB_HEADS_PER_GROUP = MB_HEADS // MB_GROUPS
MB_CONV = 5
MB_BC_WIDTH = MB_GROUPS * MB_STATE
MB_CONV_DIM = MB_INNER + 2 * MB_BC_WIDTH
MB_GROUP_WIDTH = MB_INNER // MB_GROUPS
FFN_HIDDEN = ((8 * D_MODEL + 3 * 256 - 1) // (3 * 256)) * 256
IN_SIZES = (HG_WIDTH,) * 5 + (AT_WIDTH, AT_KV_WIDTH, AT_KV_WIDTH) + (
    MB_INNER, MB_CONV_DIM, 2 * MB_HEADS) + (3 * D_MODEL,)
LN_EPS = 1e-5
RMS_EPS = 1e-6

LANES = 128
SUBLANES = 8
VMEM_LIMIT_BYTES = 56 * 1024 * 1024

ROW_TILE = 256
HG_CHUNK = 32
MB_CHUNK = 128
MODS_ROWS = 24
XDT_WIDTH = MB_CONV_DIM + LANES
CONV_CH_TILE = 512


def _cparams(*sem):
    return pltpu.CompilerParams(dimension_semantics=sem, vmem_limit_bytes=VMEM_LIMIT_BYTES)


def _resident(shape):
    nd = len(shape)
    return pl.BlockSpec(shape, lambda *_: (0,) * nd, pipeline_mode=pl.Buffered(1))


def _sigmoid(x):
    return 1.0 / (1.0 + jnp.exp(-x))


def _silu(x):
    return x * _sigmoid(x)


def _split_bf16(x):
    hi = x.astype(BF16)
    lo = (x - hi.astype(F32)).astype(BF16)
    return hi, lo


def _dot(a, b):
    return jnp.dot(a, b, preferred_element_type=F32)


def _dot_nt(a, b):
    return lax.dot_general(a, b, (((1,), (1,)), ((), ())), preferred_element_type=F32)


def _dot_tn(a, b):
    return lax.dot_general(a, b, (((0,), (0,)), ((), ())), preferred_element_type=F32)


def _dot2(m_bf16, x):
    hi, lo = _split_bf16(x)
    return _dot(m_bf16, hi) + _dot(m_bf16, lo)


def _dot2_right(x, m_bf16):
    hi, lo = _split_bf16(x)
    return _dot(hi, m_bf16) + _dot(lo, m_bf16)


def _layer_norm(v, g, b):
    mu = jnp.mean(v, axis=-1, keepdims=True)
    c = v - mu
    var = jnp.mean(c * c, axis=-1, keepdims=True)
    return c * lax.rsqrt(var + LN_EPS) * g + b


def _lbs_kernel(lb_ref, o_ref, *, depth):
    x = lb_ref[...]
    m = jnp.max(x, axis=0, keepdims=True)
    e = jnp.exp(x - m)
    p = e / jnp.sum(e, axis=0, keepdims=True)
    acc = p[0:1]
    for l in range(depth):
        if l > 0:
            acc = acc + p[l:l + 1]
        o_ref[l:l + 1, :] = acc - p[0:1]


def _lower_bounds(hg_lb):
    depth = hg_lb.shape[0]
    flat = hg_lb.reshape(depth, 2 * HG_WIDTH)
    out = pl.pallas_call(
        functools.partial(_lbs_kernel, depth=depth),
        out_shape=jax.ShapeDtypeStruct(flat.shape, F32),
        name="hg_lower_bounds",
    )(flat)
    return out.reshape(depth, 2, 1, HG_WIDTH)


def _mods_kernel(c_ref, w_ref, b_ref, o_ref):
    s = _silu(c_ref[...])
    o_ref[0] = _dot(s.astype(BF16), w_ref[0]) + b_ref[0]


def _modulations(cond, w_mod, b_mod):
    depth = w_mod.shape[0]
    nblk = w_mod.shape[2] // D_MODEL
    return pl.pallas_call(
        _mods_kernel,
        grid=(depth, nblk),
        in_specs=[
            pl.BlockSpec((MODS_ROWS, D_MODEL), lambda l, j: (0, 0)),
            pl.BlockSpec((1, D_MODEL, D_MODEL), lambda l, j: (l, 0, j)),
            pl.BlockSpec((1, 1, D_MODEL), lambda l, j: (l, 0, j)),
        ],
        out_specs=pl.BlockSpec((1, MODS_ROWS, D_MODEL), lambda l, j: (l, 0, j)),
        out_shape=jax.ShapeDtypeStruct((depth, MODS_ROWS, nblk * D_MODEL), F32),
        compiler_params=_cparams("arbitrary", "arbitrary"),
        name="adaln_modulations",
    )(cond, w_mod, b_mod.reshape(depth, 1, -1))


def _mod_spec(chunk, nct):
    return pl.BlockSpec((1, 1, D_MODEL),
                        lambda b, i: (2 * b + jnp.where(i >= nct, 1, 0), 0, chunk))


def _inproj_kernel(x_ref, sc_ref, sh_ref, w_ref, o_ref):
    h = x_ref[0] * (1.0 + sc_ref[0]) + sh_ref[0]
    o_ref[0] = _dot(h.astype(BF16), w_ref[...]).astype(o_ref.dtype)


def _inproj(xs, modsel, w, nct, name):
    bsz, s, _ = xs.shape
    n = w.shape[1]
    return pl.pallas_call(
        _inproj_kernel,
        grid=(bsz, s // ROW_TILE),
        in_specs=[
            pl.BlockSpec((1, ROW_TILE, D_MODEL), lambda b, i: (b, i, 0)),
            _mod_spec(1, nct), _mod_spec(0, nct),
            _resident(w.shape),
        ],
        out_specs=pl.BlockSpec((1, ROW_TILE, n), lambda b, i: (b, i, 0)),
        out_shape=jax.ShapeDtypeStruct((bsz, s, n), F32),
        compiler_params=_cparams("parallel", "arbitrary"),
        name=name,
    )(xs, modsel, modsel, w)


def _inproj_attn_kernel(x_ref, sc_ref, sh_ref, w_ref, cos_ref, sin_ref, qn_ref, kn_ref,
                        q_ref, k_ref, v_ref):
    h = x_ref[0] * (1.0 + sc_ref[0]) + sh_ref[0]
    p = _dot(h.astype(BF16), w_ref[...])
    cos = cos_ref[...]
    sin = sin_ref[...]
    lane = lax.broadcasted_iota(jnp.int32, cos.shape, 1)
    even = (lane % 2) == 0

    def norm_rope(xh, w):
        ms = jnp.mean(xh * xh, axis=-1, keepdims=True)
        xn = xh * lax.rsqrt(ms + RMS_EPS) * w
        partner = jnp.where(even, pltpu.roll(xn, AT_HEAD_DIM - 1, 1), pltpu.roll(xn, 1, 1))
        return xn * cos + partner * sin

    scale = AT_HEAD_DIM ** -0.5
    for hh in range(AT_HEADS):
        sl = slice(hh * AT_HEAD_DIM, (hh + 1) * AT_HEAD_DIM)
        q_ref[0, :, sl] = (norm_rope(p[:, sl], qn_ref[...]) * scale).astype(BF16)
    for kv in range(AT_KV_HEADS):
        sl = slice(kv * AT_HEAD_DIM, (kv + 1) * AT_HEAD_DIM)
        src = slice(AT_WIDTH + kv * AT_HEAD_DIM, AT_WIDTH + (kv + 1) * AT_HEAD_DIM)
        k_ref[0, :, sl] = norm_rope(p[:, src], kn_ref[...]).astype(BF16)
    v_ref[0] = p[:, AT_WIDTH + AT_KV_WIDTH:].astype(BF16)


def _inproj_attn(xs, modsel, w, cos, sin, qn, kn, nct):
    bsz, s, _ = xs.shape
    row = lambda width: pl.BlockSpec((1, ROW_TILE, width), lambda b, i: (b, i, 0))
    return pl.pallas_call(
        _inproj_attn_kernel,
        grid=(bsz, s // ROW_TILE),
        in_specs=[
            row(D_MODEL), _mod_spec(1, nct), _mod_spec(0, nct),
            _resident(w.shape),
            pl.BlockSpec((ROW_TILE, AT_HEAD_DIM), lambda b, i: (i, 0)),
            pl.BlockSpec((ROW_TILE, AT_HEAD_DIM), lambda b, i: (i, 0)),
            _resident((1, AT_HEAD_DIM)), _resident((1, AT_HEAD_DIM)),
        ],
        out_specs=[row(AT_WIDTH), row(AT_KV_WIDTH), row(AT_KV_WIDTH)],
        out_shape=[jax.ShapeDtypeStruct((bsz, s, AT_WIDTH), BF16),
                   jax.ShapeDtypeStruct((bsz, s, AT_KV_WIDTH), BF16),
                   jax.ShapeDtypeStruct((bsz, s, AT_KV_WIDTH), BF16)],
        compiler_params=_cparams("parallel", "arbitrary"),
        name="inproj_attn",
    )(xs, modsel, modsel, w, cos, sin, qn.reshape(1, -1), kn.reshape(1, -1))


def _attn_kernel(q_ref, k_ref, v_ref, o_ref, *, nct, ctx_len):
    i = pl.program_id(2)

    def attend(keys, vals):
        for hh in range(AT_GROUP):
            sl = slice(hh * AT_HEAD_DIM, (hh + 1) * AT_HEAD_DIM)
            s = _dot_nt(q_ref[0, :, sl], keys)
            m = jnp.max(s, axis=-1, keepdims=True)
            e = jnp.exp(s - m)
            l = jnp.sum(e, axis=-1, keepdims=True)
            o = _dot(e.astype(BF16), vals) / l
            o_ref[0, :, sl] = o.astype(o_ref.dtype)

    @pl.when(i < nct)
    def _():
        attend(k_ref[0, :ctx_len, :], v_ref[0, :ctx_len, :])

    @pl.when(i >= nct)
    def _():
        attend(k_ref[0], v_ref[0])


def _attention(q, k, v, nct, ctx_len):
    bsz, s, _ = q.shape
    gw = AT_GROUP * AT_HEAD_DIM
    kv_spec = pl.BlockSpec((1, s, AT_HEAD_DIM), lambda b, g, i: (b, 0, g))
    return pl.pallas_call(
        functools.partial(_attn_kernel, nct=nct, ctx_len=ctx_len),
        grid=(bsz, AT_KV_HEADS, s // ROW_TILE),
        in_specs=[pl.BlockSpec((1, ROW_TILE, gw), lambda b, g, i: (b, i, g)), kv_spec, kv_spec],
        out_specs=pl.BlockSpec((1, ROW_TILE, gw), lambda b, g, i: (b, i, g)),
        out_shape=jax.ShapeDtypeStruct((bsz, s, AT_WIDTH), BF16),
        compiler_params=_cparams("parallel", "parallel", "arbitrary"),
        name="gqa_attention",
    )(q, k, v)


def _visit_to_tile(p, v, nct, nt):
    backward = jnp.where(v < nct, nct - 1 - v, nt - 1 - (v - nct))
    return jnp.where(p == 1, v, backward)


def _hgrn2_kernel(q_ref, f_ref, i_ref, g_ref, lb_ref, gn_ref, tri_ref, o_ref,
                  qs_s, kk_s, b_s, v_s, st_s, acc_s, *, nct, nt):
    p = pl.program_id(1)
    v = pl.program_id(2)
    row0 = pl.multiple_of(_visit_to_tile(p, v, nct, nt) * ROW_TILE, ROW_TILE)
    fwd = p == 1
    n_chunks = ROW_TILE // HG_CHUNK

    @pl.when(v == 0)
    def _():
        st_s[...] = jnp.zeros_like(st_s)

    xq = q_ref[0]
    qs_s[...] = _silu(xq) * (HG_DK ** -0.5)
    lb = lb_ref[0]
    xf = f_ref[0]
    e = jnp.exp(-jnp.abs(xf))
    r = 1.0 / (1.0 + e)
    sig_pos = jnp.where(xf >= 0, r, e * r)
    sig_neg = jnp.where(xf >= 0, e * r, r)
    log_f = jnp.log(lb + (1.0 - lb) * sig_pos)
    kk_s[...] = (1.0 - lb) * sig_neg
    b_s[...] = _dot2(tri_ref[0], log_f)
    v_s[...] = i_ref[0].astype(BF16)

    trow = lax.broadcasted_iota(jnp.int32, (HG_CHUNK, HG_CHUNK), 0)
    tcol = lax.broadcasted_iota(jnp.int32, (HG_CHUNK, HG_CHUNK), 1)
    causal = jnp.where(fwd, tcol - trow, trow - tcol) <= 0
    mid_off = jnp.where(fwd, HG_CHUNK // 2 - 1, HG_CHUNK // 2)
    end_off = jnp.where(fwd, HG_CHUNK - 1, 0)

    def chunk_body(ci, carry):
        c = jnp.where(fwd, ci, n_chunks - 1 - ci)
        r0 = pl.multiple_of(c * HG_CHUNK, HG_CHUNK)
        rows = pl.ds(r0, HG_CHUNK)
        b = b_s[rows, :]
        b_mid = b_s[pl.ds(r0 + mid_off, 1), :]
        b_end = b_s[pl.ds(r0 + end_off, 1), :]
        qq = qs_s[rows, :]
        kk = kk_s[rows, :]
        q_in = (qq * jnp.exp(b - b_mid)).astype(BF16)
        k_in = (kk * jnp.exp(b_mid - b)).astype(BF16)
        q_st = (qq * jnp.exp(b)).astype(BF16)
        k_st = (kk * jnp.exp(b_end - b)).astype(BF16)
        decay = jnp.exp(b_end)
        vv = v_s[rows, :]
        acc_rows = pl.ds(row0 + r0, HG_CHUNK)
        outs, states = [], []
        for h in range(HG_HEADS):
            sl = slice(h * HG_DK, (h + 1) * HG_DK)
            a = jnp.where(causal, _dot_nt(q_in[:, sl], k_in[:, sl]), 0.0).astype(BF16)
            st = st_s[h]
            outs.append(_dot(a, vv[:, sl]) + _dot_nt(q_st[:, sl], st.astype(BF16)))
            states.append(st * decay[:, sl] + _dot_tn(vv[:, sl], k_st[:, sl]))
        for h in range(HG_HEADS):
            st_s[h] = states[h]
        acc_s[acc_rows, :] = acc_s[acc_rows, :] + jnp.concatenate(outs, axis=1)
        return carry

    @pl.when(p == 0)
    def _():
        acc_s[pl.ds(row0, ROW_TILE), :] = jnp.zeros((ROW_TILE, HG_WIDTH), F32)

    lax.fori_loop(0, n_chunks, chunk_body, 0, unroll=True)

    @pl.when(p == 1)
    def _():
        gate = _silu(g_ref[0])
        for h in range(HG_HEADS):
            sl = slice(h * HG_DV, (h + 1) * HG_DV)
            o = acc_s[pl.ds(row0, ROW_TILE), sl]
            ms = jnp.mean(o * o, axis=-1, keepdims=True)
            o_ref[0, :, sl] = (o * lax.rsqrt(ms + RMS_EPS) * gn_ref[...] * gate[:, sl]).astype(o_ref.dtype)


def _hgrn2(p_hg, lb, gnorm, tri, nct):
    bsz, s, _ = p_hg.shape
    nt = s // ROW_TILE
    tile = lambda p, v: _visit_to_tile(p, v, nct, nt)
    col = lambda c: pl.BlockSpec((1, ROW_TILE, HG_WIDTH), lambda b, p, v: (b, tile(p, v), c))
    return pl.pallas_call(
        functools.partial(_hgrn2_kernel, nct=nct, nt=nt),
        grid=(bsz, 2, nt),
        in_specs=[
            col(0),
            pl.BlockSpec((1, ROW_TILE, HG_WIDTH), lambda b, p, v: (b, tile(p, v), 2 - p)),
            col(3),
            pl.BlockSpec((1, ROW_TILE, HG_WIDTH), lambda b, p, v: (b, tile(p, v) * p, 4)),
            pl.BlockSpec((1, 1, HG_WIDTH), lambda b, p, v: (1 - p, 0, 0)),
            pl.BlockSpec((1, HG_DV), lambda b, p, v: (0, 0)),
            pl.BlockSpec((1, ROW_TILE, ROW_TILE), lambda b, p, v: (p, 0, 0)),
        ],
        out_specs=pl.BlockSpec((1, ROW_TILE, HG_WIDTH), lambda b, p, v: (b, tile(p, v) * p, 0)),
        out_shape=jax.ShapeDtypeStruct((bsz, s, HG_WIDTH), BF16),
        scratch_shapes=[
            pltpu.VMEM((ROW_TILE, HG_WIDTH), F32),
            pltpu.VMEM((ROW_TILE, HG_WIDTH), F32),
            pltpu.VMEM((ROW_TILE, HG_WIDTH), F32),
            pltpu.VMEM((ROW_TILE, HG_WIDTH), BF16),
            pltpu.VMEM((HG_HEADS, HG_DV, HG_DK), F32),
            pltpu.VMEM((s, HG_WIDTH), F32),
        ],
        compiler_params=_cparams("parallel", "arbitrary", "arbitrary"),
        name="hgrn2_bidir",
    )(p_hg, p_hg, p_hg, p_hg, lb, gnorm.reshape(1, -1), tri)


def _conv_kernel(x_ref, w_ref, b_ref, o_ref, pad_s, *, regions):
    half = MB_CONV // 2
    w = w_ref[...]
    bias = b_ref[...]
    zeros = jnp.zeros((SUBLANES, x_ref.shape[2]), F32)
    for r0, n in regions:
        pad_s[0:SUBLANES, :] = zeros
        pad_s[SUBLANES:SUBLANES + n, :] = x_ref[0, r0:r0 + n, :]
        pad_s[SUBLANES + n:2 * SUBLANES + n, :] = zeros
        acc = bias + w[0:1] * pad_s[SUBLANES - half:SUBLANES - half + n, :]
        for j in range(1, MB_CONV):
            acc = acc + w[j:j + 1] * pad_s[SUBLANES - half + j:SUBLANES - half + j + n, :]
        o_ref[0, r0:r0 + n, :] = _silu(acc)


def _conv_silu(p_xdt, conv_w, conv_b, ctx_len):
    bsz, s, _ = p_xdt.shape
    regions = ((0, ctx_len), (ctx_len, s - ctx_len))
    blk = pl.BlockSpec((1, s, CONV_CH_TILE), lambda b, j: (b, 0, j))
    return pl.pallas_call(
        functools.partial(_conv_kernel, regions=regions),
        grid=(bsz, MB_CONV_DIM // CONV_CH_TILE),
        in_specs=[blk,
                  pl.BlockSpec((MB_CONV, CONV_CH_TILE), lambda b, j: (0, j)),
                  pl.BlockSpec((1, CONV_CH_TILE), lambda b, j: (0, j))],
        out_specs=blk,
        out_shape=jax.ShapeDtypeStruct((bsz, s, MB_CONV_DIM), F32),
        scratch_shapes=[pltpu.VMEM((s + 2 * SUBLANES, CONV_CH_TILE), F32)],
        compiler_params=_cparams("parallel", "arbitrary"),
        name="ssd_conv_silu",
    )(p_xdt, conv_w, conv_b.reshape(1, -1))


def _ssd_kernel(u_ref, dt_ref, z_ref, dtb_ref, alog_ref, dexp_ref, nw_ref, tril_ref, triu_ref,
                exp_ref, o_ref, st_s, acc_s, *, nct, nt):
    p = pl.program_id(1)
    v = pl.program_id(2)
    row0 = pl.multiple_of(_visit_to_tile(p, v, nct, nt) * MB_CHUNK, MB_CHUNK)
    fwd = p == 1
    L = MB_CHUNK

    @pl.when(v == 0)
    def _():
        st_s[...] = jnp.zeros_like(st_s)

    u = u_ref[0]
    xs = u[:, :MB_INNER]
    bm = u[:, MB_INNER:MB_INNER + MB_BC_WIDTH].astype(BF16)
    cm = u[:, MB_INNER + MB_BC_WIDTH:].astype(BF16)

    xdt = dt_ref[0] + dtb_ref[...]
    dt = jnp.maximum(xdt, 0.0) + jnp.log(1.0 + jnp.exp(-jnp.abs(xdt)))
    dta = dt * (-jnp.exp(alog_ref[...]))
    cum_f = _dot2(tril_ref[...], dta)
    cum_b = _dot2(triu_ref[...], dta)
    cum = jnp.where(fwd, cum_f, cum_b)
    cum_end = jnp.where(fwd, cum[L - 1:L, :], cum[0:1, :])
    expand = exp_ref[0]
    e_in = _dot2_right(jnp.exp(cum), expand)
    e_out = _dot2_right(jnp.exp(cum_end - cum) * dt, expand)
    e_end = jnp.where(fwd, e_in[L - 1:L, :], e_in[0:1, :])
    x_out = (xs * e_out).astype(BF16)

    y_parts = []
    for g in range(MB_GROUPS):
        gsl = slice(g * MB_GROUP_WIDTH, (g + 1) * MB_GROUP_WIDTH)
        nsl = slice(g * MB_STATE, (g + 1) * MB_STATE)
        st = st_s[:, gsl]
        y_parts.append(_dot(cm[:, nsl], st.astype(BF16)) * e_in[:, gsl])
        st_s[:, gsl] = st * e_end[:, gsl] + _dot_tn(bm[:, nsl], x_out[:, gsl])
    y_inter = jnp.concatenate(y_parts, axis=1)

    rows = pl.ds(row0, L)

    @pl.when(p == 0)
    def _():
        acc_s[rows, :] = y_inter

    @pl.when(p == 1)
    def _():
        trow = lax.broadcasted_iota(jnp.int32, (L, L), 0)
        tcol = lax.broadcasted_iota(jnp.int32, (L, L), 1)
        lower = tcol <= trow
        upper = tcol >= trow
        lane = lax.broadcasted_iota(jnp.int32, (L, LANES), 1)
        first_half = lane < MB_HEAD_DIM
        cum_f_t = cum_f.T
        cum_b_t = cum_b.T
        dt_t = dt.T
        xs16 = xs.astype(BF16)
        neg_inf = jnp.float32(-jnp.inf)
        pieces = []
        for g in range(MB_GROUPS):
            nsl = slice(g * MB_STATE, (g + 1) * MB_STATE)
            cb = _dot_nt(cm[:, nsl], bm[:, nsl])
            for hp in range(MB_HEADS_PER_GROUP // 2):
                ys = []
                for k in range(2):
                    h = g * MB_HEADS_PER_GROUP + 2 * hp + k
                    hb = MB_HEADS + h
                    seg_f = cum_f[:, h:h + 1] - cum_f_t[h:h + 1, :]
                    seg_b = cum_b[:, hb:hb + 1] - cum_b_t[hb:hb + 1, :]
                    m_f = jnp.exp(jnp.where(lower, seg_f, neg_inf)) * dt_t[h:h + 1, :]
                    m_b = jnp.exp(jnp.where(upper, seg_b, neg_inf)) * dt_t[hb:hb + 1, :]
                    m = (cb * (m_f + m_b)).astype(BF16)
                    c0 = (h // 2) * LANES
                    ys.append(_dot(m, xs16[:, c0:c0 + LANES]))
                pieces.append(jnp.where(first_half, ys[0], ys[1]))
        y = jnp.concatenate(pieces, axis=1) + y_inter + acc_s[rows, :]
        y = (y + dexp_ref[...] * xs) * _silu(z_ref[0])
        for g in range(MB_GROUPS):
            gsl = slice(g * MB_GROUP_WIDTH, (g + 1) * MB_GROUP_WIDTH)
            yg = y[:, gsl]
            ms = jnp.mean(yg * yg, axis=-1, keepdims=True)
            o_ref[0, :, gsl] = (yg * lax.rsqrt(ms + RMS_EPS) * nw_ref[:, gsl]).astype(o_ref.dtype)


def _ssd(u, p_xdt, z, dt_bias_row, a_log_row, d_exp, norm_w, tril, triu, expand, nct):
    bsz, s, _ = u.shape
    nt = s // MB_CHUNK
    tile = lambda p, v: _visit_to_tile(p, v, nct, nt)
    dt_block = MB_CONV_DIM // LANES
    return pl.pallas_call(
        functools.partial(_ssd_kernel, nct=nct, nt=nt),
        grid=(bsz, 2, nt),
        in_specs=[
            pl.BlockSpec((1, MB_CHUNK, MB_CONV_DIM), lambda b, p, v: (b, tile(p, v), 0)),
            pl.BlockSpec((1, MB_CHUNK, LANES), lambda b, p, v: (b, tile(p, v), dt_block)),
            pl.BlockSpec((1, MB_CHUNK, MB_INNER), lambda b, p, v: (b, tile(p, v) * p, 0)),
            pl.BlockSpec((1, LANES), lambda b, p, v: (0, 0)),
            pl.BlockSpec((1, LANES), lambda b, p, v: (0, 0)),
            pl.BlockSpec((1, MB_INNER), lambda b, p, v: (0, 0)),
            pl.BlockSpec((1, MB_INNER), lambda b, p, v: (0, 0)),
            pl.BlockSpec((MB_CHUNK, MB_CHUNK), lambda b, p, v: (0, 0)),
            pl.BlockSpec((MB_CHUNK, MB_CHUNK), lambda b, p, v: (0, 0)),
            pl.BlockSpec((1, LANES, MB_INNER), lambda b, p, v: (p, 0, 0)),
        ],
        out_specs=pl.BlockSpec((1, MB_CHUNK, MB_INNER), lambda b, p, v: (b, tile(p, v) * p, 0)),
        out_shape=jax.ShapeDtypeStruct((bsz, s, MB_INNER), BF16),
        scratch_shapes=[
            pltpu.VMEM((MB_STATE, MB_INNER), F32),
            pltpu.VMEM((s, MB_INNER), F32),
        ],
        compiler_params=_cparams("parallel", "arbitrary", "arbitrary"),
        name="ssd_bidir",
    )(u, p_xdt, z, dt_bias_row, a_log_row, d_exp, norm_w, tril, triu, expand)


def _merge_kernel(x_ref, ohg_ref, oat_ref, omb_ref, gt_ref, g1_ref, whg_ref, wat_ref, wmb_ref,
                  wout_ref, lng_ref, lnb_ref, o_ref, *, alpha):
    gates = _sigmoid(gt_ref[0])
    y = (gates[:, :D_MODEL] * _dot(ohg_ref[0], whg_ref[...])
         + gates[:, D_MODEL:2 * D_MODEL] * _dot(oat_ref[0], wat_ref[...])
         + gates[:, 2 * D_MODEL:] * _dot(omb_ref[0], wmb_ref[...]))
    y = _dot(y.astype(BF16), wout_ref[...])
    o_ref[0] = _layer_norm(alpha * x_ref[0] + g1_ref[0] * y, lng_ref[...], lnb_ref[...])


def _merge(xs, o_hg, o_at, o_mb, gates, modsel, w_hg, w_at, w_mb, w_out, ln_g, ln_b, nct, alpha):
    bsz, s, _ = xs.shape
    row = lambda width: pl.BlockSpec((1, ROW_TILE, width), lambda b, i: (b, i, 0))
    return pl.pallas_call(
        functools.partial(_merge_kernel, alpha=alpha),
        grid=(bsz, s // ROW_TILE),
        in_specs=[row(D_MODEL), row(HG_WIDTH), row(AT_WIDTH), row(MB_INNER), row(3 * D_MODEL),
                  _mod_spec(2, nct),
                  _resident(w_hg.shape), _resident(w_at.shape), _resident(w_mb.shape),
                  _resident(w_out.shape), _resident((1, D_MODEL)), _resident((1, D_MODEL))],
        out_specs=row(D_MODEL),
        out_shape=jax.ShapeDtypeStruct(xs.shape, F32),
        compiler_params=_cparams("parallel", "arbitrary"),
        name="merge_out_ln",
    )(xs, o_hg, o_at, o_mb, gates, modsel, w_hg, w_at, w_mb, w_out,
      ln_g.reshape(1, -1), ln_b.reshape(1, -1))


def _ffn_kernel(x_ref, sc_ref, sh_ref, g2_ref, win_ref, wout_ref, lng_ref, lnb_ref, o_ref, *, alpha):
    x = x_ref[0]
    h = x * (1.0 + sc_ref[0]) + sh_ref[0]
    gu = _dot(h.astype(BF16), win_ref[...])
    a = _silu(gu[:, :FFN_HIDDEN]) * gu[:, FFN_HIDDEN:]
    y = _dot(a.astype(BF16), wout_ref[...])
    o_ref[0] = _layer_norm(alpha * x + g2_ref[0] * y, lng_ref[...], lnb_ref[...])


def _ffn(xs, modsel, w_in, w_out, ln_g, ln_b, nct, alpha):
    bsz, s, _ = xs.shape
    row = pl.BlockSpec((1, ROW_TILE, D_MODEL), lambda b, i: (b, i, 0))
    return pl.pallas_call(
        functools.partial(_ffn_kernel, alpha=alpha),
        grid=(bsz, s // ROW_TILE),
        in_specs=[row, _mod_spec(4, nct), _mod_spec(3, nct), _mod_spec(5, nct),
                  _resident(w_in.shape), _resident(w_out.shape),
                  _resident((1, D_MODEL)), _resident((1, D_MODEL))],
        out_specs=row,
        out_shape=jax.ShapeDtypeStruct(xs.shape, F32),
        compiler_params=_cparams("parallel", "arbitrary"),
        name="swiglu_ffn_ln",
    )(xs, modsel, modsel, modsel, w_in, w_out, ln_g.reshape(1, -1), ln_b.reshape(1, -1))


def _rope_tables(seq, ctx_len):
    rows = seq // GRID_W
    row, col = jnp.meshgrid(jnp.arange(rows, dtype=F32), jnp.arange(GRID_W, dtype=F32), indexing="ij")
    n_pairs = AT_HEAD_DIM // 4
    inv_freq = ROPE_THETA ** (-jnp.arange(n_pairs, dtype=F32) / n_pairs)
    ang = jnp.concatenate([row.reshape(-1, 1) * inv_freq, col.reshape(-1, 1) * inv_freq], axis=-1)
    cos = jnp.repeat(jnp.cos(ang), 2, axis=-1)
    sin = jnp.repeat(jnp.sin(ang), 2, axis=-1) * jnp.tile(jnp.array([-1.0, 1.0], F32), AT_HEAD_DIM // 2)
    cos = jnp.concatenate([jnp.ones((ctx_len, AT_HEAD_DIM), F32), cos], axis=0)
    sin = jnp.concatenate([jnp.zeros((ctx_len, AT_HEAD_DIM), F32), sin], axis=0)
    return cos, sin


def _block_tri(n, blk):
    r = jnp.arange(n)
    same = (r[:, None] // blk) == (r[None, :] // blk)
    lower = (same & (r[None, :] <= r[:, None])).astype(BF16)
    upper = (same & (r[None, :] >= r[:, None])).astype(BF16)
    return lower, upper


def _head_expand():
    lane = jnp.arange(LANES)[:, None]
    head = jnp.arange(MB_INNER)[None, :] // MB_HEAD_DIM
    fwd = (lane == head).astype(BF16)
    bwd = (lane == head + MB_HEADS).astype(BF16)
    return jnp.stack([bwd, fwd])


def _pad_lanes(row):
    return jnp.pad(row.reshape(1, -1), ((0, 0), (0, LANES - row.size)))


def kernel(x, c, ctx, c_ctx, w_mod, b_mod, w_in, hg_lb, hg_gnorm, at_qnorm, at_knorm,
           mb_conv_w, mb_conv_b, mb_dt_bias, mb_a_log, mb_d, mb_norm,
           w_br_hg, w_br_at, w_br_mb, w_out, ln1_g, ln1_b, w_ffn_in, w_ffn_out, ln2_g, ln2_b):
    bsz, seq, _ = x.shape
    ctx_len = ctx.shape[1]
    depth = w_mod.shape[0]
    assert seq % ROW_TILE == 0 and ctx_len % ROW_TILE == 0 and seq % GRID_W == 0
    assert bsz + 1 <= MODS_ROWS
    alpha = (2 * depth) ** 0.25
    nct = ctx_len // ROW_TILE
    nct_mb = ctx_len // MB_CHUNK

    cos, sin = _rope_tables(seq, ctx_len)
    hg_tril, hg_triu = _block_tri(ROW_TILE, HG_CHUNK)
    hg_tri = jnp.stack([hg_triu, hg_tril])
    mb_tril, mb_triu = _block_tri(MB_CHUNK, MB_CHUNK)
    expand = _head_expand()

    lbs = _lower_bounds(hg_lb)
    cond = jnp.concatenate([c, c_ctx[None, :], jnp.zeros((MODS_ROWS - bsz - 1, D_MODEL), F32)], axis=0)
    mods = _modulations(cond, w_mod.astype(BF16), b_mod)
    ctx_rows = jnp.broadcast_to(mods[:, bsz:bsz + 1, :], (depth, bsz, mods.shape[-1]))
    modsel_all = jnp.stack([ctx_rows, mods[:, :bsz, :]], axis=2).reshape(depth, 2 * bsz, 1, -1)

    splits = [0]
    for width in IN_SIZES:
        splits.append(splits[-1] + width)
    o_hg, o_at, o_z, o_xbc, o_dt, o_gt = splits[0], splits[5], splits[8], splits[9], splits[10], splits[11]

    xs = jnp.concatenate([ctx, x], axis=1)
    for l in range(depth):
        modsel = modsel_all[l]
        w_l = w_in[l].astype(BF16)
        w_hg_in = w_l[:, o_hg:o_at]
        w_at_in = w_l[:, o_at:o_z]
        w_z_in = w_l[:, o_z:o_xbc]
        w_xdt_in = jnp.pad(w_l[:, o_xbc:o_gt], ((0, 0), (0, XDT_WIDTH - (o_gt - o_xbc))))
        w_gt_in = w_l[:, o_gt:]

        p_hg = _inproj(xs, modsel, w_hg_in, nct, "inproj_hgrn2")
        p_z = _inproj(xs, modsel, w_z_in, nct, "inproj_ssd_z")
        p_xdt = _inproj(xs, modsel, w_xdt_in, nct, "inproj_ssd_xdt")
        p_gt = _inproj(xs, modsel, w_gt_in, nct, "inproj_gates")
        q, k, v = _inproj_attn(xs, modsel, w_at_in, cos, sin, at_qnorm[l], at_knorm[l], nct)

        out_hg = _hgrn2(p_hg, lbs[l], hg_gnorm[l], hg_tri, nct)
        out_at = _attention(q, k, v, nct, ctx_len)
        u = _conv_silu(p_xdt, mb_conv_w[l], mb_conv_b[l], ctx_len)
        out_mb = _ssd(u, p_xdt, p_z, _pad_lanes(mb_dt_bias[l]), _pad_lanes(mb_a_log[l]),
                      jnp.repeat(mb_d[l], MB_HEAD_DIM).reshape(1, -1), mb_norm[l].reshape(1, -1),
                      mb_tril, mb_triu, expand, nct_mb)

        xs = _merge(xs, out_hg, out_at, out_mb, p_gt, modsel,
                    w_br_hg[l].astype(BF16), w_br_at[l].astype(BF16), w_br_mb[l].astype(BF16),
                    w_out[l].astype(BF16), ln1_g[l], ln1_b[l], nct, alpha)
        xs = _ffn(xs, modsel, w_ffn_in[l].astype(BF16), w_ffn_out[l].astype(BF16),
                  ln2_g[l], ln2_b[l], nct, alpha)
    return xs[:, ctx_len:, :]
```

```python
import functools
import math

import jax
import jax.numpy as jnp
from jax import lax
from jax.experimental import pallas as pl
from jax.experimental.pallas import tpu as pltpu

F32 = jnp.float32
BF16 = jnp.bfloat16

D_MODEL = 1024
HG_HEADS = 8
HG_DK = 128
HG_DV = 128
HG_WIDTH = HG_HEADS * HG_DK
AT_HEADS = 8
AT_KV_HEADS = 2
AT_GROUP = AT_HEADS // AT_KV_HEADS
AT_HEAD_DIM = 128
AT_WIDTH = AT_HEADS * AT_HEAD_DIM
AT_KV_WIDTH = AT_KV_HEADS * AT_HEAD_DIM
GRID_W = 64
ROPE_THETA = 10000.0
MB_INNER = 2 * D_MODEL
MB_HEAD_DIM = 64
MB_HEADS = MB_INNER // MB_HEAD_DIM
MB_STATE = 128
MB_GROUPS = 4
MB_HEADS_PER_GROUP = MB_HEADS // MB_GROUPS
MB_CONV = 5
MB_BC_WIDTH = MB_GROUPS * MB_STATE
MB_CONV_DIM = MB_INNER + 2 * MB_BC_WIDTH
MB_GROUP_WIDTH = MB_INNER // MB_GROUPS
FFN_HIDDEN = ((8 * D_MODEL + 3 * 256 - 1) // (3 * 256)) * 256
IN_SIZES = (HG_WIDTH,) * 5 + (AT_WIDTH, AT_KV_WIDTH, AT_KV_WIDTH) + (
    MB_INNER, MB_CONV_DIM, 2 * MB_HEADS) + (3 * D_MODEL,)
LN_EPS = 1e-5
RMS_EPS = 1e-6
LOG2E = 1.4426950408889634

LANES = 128
SUBLANES = 8
VMEM_LIMIT_BYTES = 56 * 1024 * 1024

ROW_TILE = 256
HG_CHUNK = 64
HG_SUB = 32
MB_CHUNK = 128
MODS_ROWS = 24
CONV_CH_TILE = 512


def _cparams(*sem):
    return pltpu.CompilerParams(dimension_semantics=sem, vmem_limit_bytes=VMEM_LIMIT_BYTES)


def _resident(shape):
    nd = len(shape)
    return pl.BlockSpec(shape, lambda *_: (0,) * nd, pipeline_mode=pl.Buffered(1))


def _sigmoid(x):
    return 1.0 / (1.0 + jnp.exp(-x))


def _silu(x):
    return x * _sigmoid(x)


def _split_bf16(x):
    hi = x.astype(BF16)
    lo = (x - hi.astype(F32)).astype(BF16)
    return hi, lo


def _dot(a, b):
    return jnp.dot(a, b, preferred_element_type=F32)


def _dot_nt(a, b):
    return lax.dot_general(a, b, (((1,), (1,)), ((), ())), preferred_element_type=F32)


def _dot_tn(a, b):
    return lax.dot_general(a, b, (((0,), (0,)), ((), ())), preferred_element_type=F32)


def _dot2(m_bf16, x):
    hi, lo = _split_bf16(x)
    return _dot(m_bf16, hi) + _dot(m_bf16, lo)


def _dot2_right(x, m_bf16):
    hi, lo = _split_bf16(x)
    return _dot(hi, m_bf16) + _dot(lo, m_bf16)


def _layer_norm(v, g, b):
    mu = jnp.mean(v, axis=-1, keepdims=True)
    c = v - mu
    var = jnp.mean(c * c, axis=-1, keepdims=True)
    return c * lax.rsqrt(var + LN_EPS) * g + b


def _lbs_kernel(lb_ref, o_ref, *, depth):
    x = lb_ref[...]
    m = jnp.max(x, axis=0, keepdims=True)
    e = jnp.exp(x - m)
    p = e / jnp.sum(e, axis=0, keepdims=True)
    acc = p[0:1]
    for l in range(depth):
        if l > 0:
            acc = acc + p[l:l + 1]
        o_ref[l:l + 1, :] = acc - p[0:1]


def _lower_bounds(hg_lb):
    depth = hg_lb.shape[0]
    flat = hg_lb.reshape(depth, 2 * HG_WIDTH)
    out = pl.pallas_call(
        functools.partial(_lbs_kernel, depth=depth),
        out_shape=jax.ShapeDtypeStruct(flat.shape, F32),
        name="hg_lower_bounds",
    )(flat)
    return out.reshape(depth, 2, 1, HG_WIDTH)


def _mods_kernel(c_ref, w_ref, b_ref, o_ref):
    s = _silu(c_ref[...])
    o_ref[0] = _dot(s.astype(BF16), w_ref[0]) + b_ref[0]


def _modulations(cond, w_mod, b_mod):
    depth = w_mod.shape[0]
    nblk = w_mod.shape[2] // D_MODEL
    return pl.pallas_call(
        _mods_kernel,
        grid=(depth, nblk),
        in_specs=[
            pl.BlockSpec((MODS_ROWS, D_MODEL), lambda l, j: (0, 0)),
            pl.BlockSpec((1, D_MODEL, D_MODEL), lambda l, j: (l, 0, j)),
            pl.BlockSpec((1, 1, D_MODEL), lambda l, j: (l, 0, j)),
        ],
        out_specs=pl.BlockSpec((1, MODS_ROWS, D_MODEL), lambda l, j: (l, 0, j)),
        out_shape=jax.ShapeDtypeStruct((depth, MODS_ROWS, nblk * D_MODEL), F32),
        compiler_params=_cparams("arbitrary", "arbitrary"),
        name="adaln_modulations",
    )(cond, w_mod, b_mod.reshape(depth, 1, -1))


def _mod_spec(chunk, nct):
    return pl.BlockSpec((1, 1, D_MODEL),
                        lambda b, i: (2 * b + jnp.where(i >= nct, 1, 0), 0, chunk))


def _row_spec(width):
    return pl.BlockSpec((1, ROW_TILE, width), lambda b, i: (b, i, 0))


def _inproj_hg_kernel(x_ref, sc_ref, sh_ref, w_ref, lb_ref, q_ref, lf_ref, kk_ref, v_ref, g_ref):
    h = (x_ref[0] * (1.0 + sc_ref[0]) + sh_ref[0]).astype(BF16)
    W = HG_WIDTH

    def proj(j):
        return _dot(h, w_ref[:, j * W:(j + 1) * W])

    q_ref[0] = (_silu(proj(0)) * (HG_DK ** -0.5)).astype(BF16)
    for d in range(2):
        lb = lb_ref[d]
        xf = proj(1 + d)
        e = jnp.exp(-jnp.abs(xf))
        r = 1.0 / (1.0 + e)
        sig_pos = jnp.where(xf >= 0, r, e * r)
        sig_neg = jnp.where(xf >= 0, e * r, r)
        lf_ref[0, :, d * W:(d + 1) * W] = jnp.log(lb + (1.0 - lb) * sig_pos)
        kk_ref[0, :, d * W:(d + 1) * W] = ((1.0 - lb) * sig_neg).astype(BF16)
    v_ref[0] = proj(3).astype(BF16)
    g_ref[0] = _silu(proj(4)).astype(BF16)


def _inproj_hg(xs, modsel, w, lb, nct):
    bsz, s, _ = xs.shape
    shapes = [(HG_WIDTH, BF16), (2 * HG_WIDTH, F32), (2 * HG_WIDTH, BF16), (HG_WIDTH, BF16), (HG_WIDTH, BF16)]
    return pl.pallas_call(
        _inproj_hg_kernel,
        grid=(bsz, s // ROW_TILE),
        in_specs=[_row_spec(D_MODEL), _mod_spec(1, nct), _mod_spec(0, nct),
                  _resident(w.shape), _resident(lb.shape)],
        out_specs=[_row_spec(width) for width, _ in shapes],
        out_shape=[jax.ShapeDtypeStruct((bsz, s, width), dt) for width, dt in shapes],
        compiler_params=_cparams("parallel", "arbitrary"),
        name="inproj_hgrn2",
    )(xs, modsel, modsel, w, lb)


def _inproj_rest_kernel(x_ref, xp_ref, xn_ref, sc_ref, sh_ref, wat_ref, wz_ref, wgt_ref, wdt_ref, wx_ref,
                        cos_ref, sin_ref, qn_ref, kn_ref, cw_ref, cb_ref,
                        q_ref, k_ref, v_ref, z_ref, gt_ref, dt_ref, u_ref, pe_s, *, nct, nt):
    i = pl.program_id(1)
    scale1 = 1.0 + sc_ref[0]
    shift = sh_ref[0]
    xm = x_ref[0] * scale1 + shift
    h = xm.astype(BF16)

    p = _dot(h, wat_ref[...])
    cos = cos_ref[...]
    sin = sin_ref[...]
    lane = lax.broadcasted_iota(jnp.int32, cos.shape, 1)
    even = (lane % 2) == 0

    def norm_rope(xh, w):
        ms = jnp.mean(xh * xh, axis=-1, keepdims=True)
        xn = xh * lax.rsqrt(ms + RMS_EPS) * w
        partner = jnp.where(even, pltpu.roll(xn, AT_HEAD_DIM - 1, 1), pltpu.roll(xn, 1, 1))
        return xn * cos + partner * sin

    qscale = AT_HEAD_DIM ** -0.5
    for hh in range(AT_HEADS):
        sl = slice(hh * AT_HEAD_DIM, (hh + 1) * AT_HEAD_DIM)
        q_ref[0, :, sl] = (norm_rope(p[:, sl], qn_ref[...]) * qscale).astype(BF16)
    for kv in range(AT_KV_HEADS):
        sl = slice(kv * AT_HEAD_DIM, (kv + 1) * AT_HEAD_DIM)
        src = slice(AT_WIDTH + kv * AT_HEAD_DIM, AT_WIDTH + (kv + 1) * AT_HEAD_DIM)
        k_ref[0, :, sl] = norm_rope(p[:, src], kn_ref[...]).astype(BF16)
    v_ref[0] = p[:, AT_WIDTH + AT_KV_WIDTH:].astype(BF16)

    z_ref[0] = _silu(_dot(h, wz_ref[...])).astype(BF16)
    gt_ref[0] = _sigmoid(_dot(h, wgt_ref[...])).astype(BF16)
    dt_ref[0] = _dot(h, wdt_ref[...])

    half = MB_CONV // 2
    rows_ext = ROW_TILE + 2 * SUBLANES
    h_ext = jnp.concatenate([xp_ref[0] * scale1 + shift, xm, xn_ref[0] * scale1 + shift], axis=0).astype(BF16)
    prev_ok = jnp.where((i != 0) & (i != nct), 1.0, 0.0)
    next_ok = jnp.where((i != nct - 1) & (i != nt - 1), 1.0, 0.0)
    rowi = lax.broadcasted_iota(jnp.int32, (rows_ext, CONV_CH_TILE), 0)
    halo = jnp.where(rowi < SUBLANES, prev_ok, jnp.where(rowi >= ROW_TILE + SUBLANES, next_ok, 1.0))
    for c in range(MB_CONV_DIM // CONV_CH_TILE):
        csl = slice(c * CONV_CH_TILE, (c + 1) * CONV_CH_TILE)
        pe_s[...] = _dot(h_ext, wx_ref[:, csl]) * halo
        acc = cb_ref[:, csl] + cw_ref[0:1, csl] * pe_s[SUBLANES - half:SUBLANES - half + ROW_TILE, :]
        for j in range(1, MB_CONV):
            acc = acc + cw_ref[j:j + 1, csl] * pe_s[SUBLANES - half + j:SUBLANES - half + j + ROW_TILE, :]
        u_ref[0, :, csl] = _silu(acc).astype(BF16)


def _inproj_rest(xs, modsel, w_at, w_z, w_gt, w_dt, w_xbc, cos, sin, qn, kn, conv_w, conv_b, nct):
    bsz, s, _ = xs.shape
    nt = s // ROW_TILE
    halo_blocks = ROW_TILE // SUBLANES
    last_halo = s // SUBLANES - 1
    shapes = [(AT_WIDTH, BF16), (AT_KV_WIDTH, BF16), (AT_KV_WIDTH, BF16), (MB_INNER, BF16),
              (3 * D_MODEL, BF16), (LANES, F32), (MB_CONV_DIM, BF16)]
    return pl.pallas_call(
        functools.partial(_inproj_rest_kernel, nct=nct, nt=nt),
        grid=(bsz, nt),
        in_specs=[
            _row_spec(D_MODEL),
            pl.BlockSpec((1, SUBLANES, D_MODEL), lambda b, i: (b, jnp.maximum(i * halo_blocks - 1, 0), 0)),
            pl.BlockSpec((1, SUBLANES, D_MODEL), lambda b, i: (b, jnp.minimum((i + 1) * halo_blocks, last_halo), 0)),
            _mod_spec(1, nct), _mod_spec(0, nct),
            _resident(w_at.shape), _resident(w_z.shape), _resident(w_gt.shape), _resident(w_dt.shape),
            _resident(w_xbc.shape),
            pl.BlockSpec((ROW_TILE, AT_HEAD_DIM), lambda b, i: (i, 0)),
            pl.BlockSpec((ROW_TILE, AT_HEAD_DIM), lambda b, i: (i, 0)),
            _resident((1, AT_HEAD_DIM)), _resident((1, AT_HEAD_DIM)),
            _resident(conv_w.shape), _resident((1, MB_CONV_DIM)),
        ],
        out_specs=[_row_spec(width) for width, _ in shapes],
        out_shape=[jax.ShapeDtypeStruct((bsz, s, width), dt) for width, dt in shapes],
        scratch_shapes=[pltpu.VMEM((ROW_TILE + 2 * SUBLANES, CONV_CH_TILE), F32)],
        compiler_params=_cparams("parallel", "arbitrary"),
        name="inproj_attn_ssd_gates",
    )(xs, xs, xs, modsel, modsel, w_at, w_z, w_gt, w_dt, w_xbc, cos, sin,
      qn.reshape(1, -1), kn.reshape(1, -1), conv_w, conv_b.reshape(1, -1))


def _attn_kernel(q_ref, k_ref, v_ref, o_ref, *, nct, ctx_len):
    i = pl.program_id(2)

    def attend(keys, vals):
        for hh in range(AT_GROUP):
            sl = slice(hh * AT_HEAD_DIM, (hh + 1) * AT_HEAD_DIM)
            s = _dot_nt(q_ref[0, :, sl], keys)
            m = jnp.max(s, axis=-1, keepdims=True)
            e = jnp.exp(s - m)
            l = jnp.sum(e, axis=-1, keepdims=True)
            o = _dot(e.astype(BF16), vals) / l
            o_ref[0, :, sl] = o.astype(o_ref.dtype)

    @pl.when(i < nct)
    def _():
        attend(k_ref[0, :ctx_len, :], v_ref[0, :ctx_len, :])

    @pl.when(i >= nct)
    def _():
        attend(k_ref[0], v_ref[0])


def _attention(q, k, v, nct, ctx_len):
    bsz, s, _ = q.shape
    gw = AT_GROUP * AT_HEAD_DIM
    kv_spec = pl.BlockSpec((1, s, AT_HEAD_DIM), lambda b, g, i: (b, 0, g))
    return pl.pallas_call(
        functools.partial(_attn_kernel, nct=nct, ctx_len=ctx_len),
        grid=(bsz, AT_KV_HEADS, s // ROW_TILE),
        in_specs=[pl.BlockSpec((1, ROW_TILE, gw), lambda b, g, i: (b, i, g)), kv_spec, kv_spec],
        out_specs=pl.BlockSpec((1, ROW_TILE, gw), lambda b, g, i: (b, i, g)),
        out_shape=jax.ShapeDtypeStruct((bsz, s, AT_WIDTH), BF16),
        compiler_params=_cparams("parallel", "parallel", "arbitrary"),
        name="gqa_attention",
    )(q, k, v)


def _visit_to_tile(p, v, nct, nt):
    backward = jnp.where(v < nct, nct - 1 - v, nt - 1 - (v - nct))
    return jnp.where(p == 1, v, backward)


def _hgrn2_kernel(q_ref, lf_ref, kk_ref, v_ref, g_ref, gn_ref, tri_ref, o_ref,
                  b_s, st_s, acc_s, *, nct, nt):
    p = pl.program_id(1)
    v = pl.program_id(2)
    row0 = pl.multiple_of(_visit_to_tile(p, v, nct, nt) * ROW_TILE, ROW_TILE)

    @pl.when(v == 0)
    def _():
        st_s[...] = jnp.zeros_like(st_s)

    b_s[...] = _dot2(tri_ref[0], lf_ref[0])

    @pl.when(p == 0)
    def _():
        _hgrn2_tile(False, row0, q_ref, kk_ref, v_ref, b_s, st_s, acc_s)

    @pl.when(p == 1)
    def _():
        _hgrn2_tile(True, row0, q_ref, kk_ref, v_ref, b_s, st_s, acc_s)
        gate = g_ref[0].astype(F32)
        for h in range(HG_HEADS):
            sl = slice(h * HG_DV, (h + 1) * HG_DV)
            o = acc_s[pl.ds(row0, ROW_TILE), sl]
            ms = jnp.mean(o * o, axis=-1, keepdims=True)
            o_ref[0, :, sl] = (o * lax.rsqrt(ms + RMS_EPS) * gn_ref[...] * gate[:, sl]).astype(o_ref.dtype)


def _hgrn2_tile(fwd, row0, q_ref, kk_ref, v_ref, b_s, st_s, acc_s):
    C, SB, W = HG_CHUNK, HG_SUB, HG_WIDTH
    nb = C // SB
    n_chunks = ROW_TILE // C
    end_r = SB - 1 if fwd else 0
    mid_r = SB // 2 - 1 if fwd else SB // 2
    trow = lax.broadcasted_iota(jnp.int32, (C, C), 0)
    tcol = lax.broadcasted_iota(jnp.int32, (C, C), 1)
    blk_gap = (trow // SB - tcol // SB) if fwd else (tcol // SB - trow // SB)
    same_blk = (blk_gap == 0) & ((tcol <= trow) if fwd else (tcol >= trow))
    order = list(range(nb)) if fwd else list(reversed(range(nb)))

    def per_block(rows):
        return jnp.concatenate([jnp.broadcast_to(x, (SB, W)) for x in rows], axis=0)

    zero_row = jnp.zeros((1, W), F32)
    for c in (range(n_chunks) if fwd else reversed(range(n_chunks))):
        r0 = c * C
        u = b_s[r0:r0 + C, :]
        d = [b_s[r0 + R * SB + end_r:r0 + R * SB + end_r + 1, :] for R in range(nb)]
        mid = per_block([b_s[r0 + R * SB + mid_r:r0 + R * SB + mid_r + 1, :] for R in range(nb)])
        d_full = per_block(d)
        before = {}
        run = zero_row
        for R in order:
            before[R] = run
            run = run + d[R]
        total = run
        e_before = per_block([jnp.exp(before[R]) for R in range(nb)])
        e_after = per_block([jnp.exp(total - before[R] - d[R]) for R in range(nb)])

        qq = q_ref[0, r0:r0 + C, :].astype(F32)
        kk = kk_ref[0, r0:r0 + C, :].astype(F32)
        q_in = (qq * jnp.exp(u - mid)).astype(BF16)
        k_in = (kk * jnp.exp(mid - u)).astype(BF16)
        q_blk = qq * jnp.exp(u)
        k_blk = kk * jnp.exp(d_full - u)
        k_blk16 = k_blk.astype(BF16)
        q_state = (q_blk * e_before).astype(BF16)
        k_state = (k_blk * e_after).astype(BF16)
        decay = jnp.exp(total)
        q_gap = {1: q_blk.astype(BF16)}
        for gap in range(2, nb):
            between = []
            for R in range(nb):
                J = R - gap if fwd else R + gap
                if 0 <= J < nb:
                    mids = range(J + 1, R) if fwd else range(R + 1, J)
                    g = zero_row
                    for m in mids:
                        g = g + d[m]
                    between.append(g)
                else:
                    between.append(zero_row)
            q_gap[gap] = (q_blk * per_block([jnp.exp(g) for g in between])).astype(BF16)
        vv = v_ref[0, r0:r0 + C, :]

        outs, states = [], []
        for h in range(HG_HEADS):
            sl = slice(h * HG_DK, (h + 1) * HG_DK)
            a = jnp.where(same_blk, _dot_nt(q_in[:, sl], k_in[:, sl]), 0.0)
            for gap in range(1, nb):
                a = a + jnp.where(blk_gap == gap, _dot_nt(q_gap[gap][:, sl], k_blk16[:, sl]), 0.0)
            st = st_s[h]
            outs.append(_dot(a.astype(BF16), vv[:, sl]) + _dot_nt(q_state[:, sl], st.astype(BF16)))
            states.append(st * decay[:, sl] + _dot_tn(vv[:, sl], k_state[:, sl]))
        for h in range(HG_HEADS):
            st_s[h] = states[h]
        o = jnp.concatenate(outs, axis=1)
        acc_rows = pl.ds(row0 + r0, C)
        if fwd:
            acc_s[acc_rows, :] = acc_s[acc_rows, :] + o
        else:
            acc_s[acc_rows, :] = o


def _hgrn2(qs, lf, kk, vals, gate, gnorm, tri, nct):
    bsz, s, _ = qs.shape
    nt = s // ROW_TILE
    tile = lambda p, v: _visit_to_tile(p, v, nct, nt)
    both = pl.BlockSpec((1, ROW_TILE, HG_WIDTH), lambda b, p, v: (b, tile(p, v), 0))
    by_dir = pl.BlockSpec((1, ROW_TILE, HG_WIDTH), lambda b, p, v: (b, tile(p, v), 1 - p))
    pass1 = pl.BlockSpec((1, ROW_TILE, HG_WIDTH), lambda b, p, v: (b, tile(p, v) * p, 0))
    return pl.pallas_call(
        functools.partial(_hgrn2_kernel, nct=nct, nt=nt),
        grid=(bsz, 2, nt),
        in_specs=[
            both, by_dir, by_dir, both, pass1,
            pl.BlockSpec((1, HG_DV), lambda b, p, v: (0, 0)),
            pl.BlockSpec((1, ROW_TILE, ROW_TILE), lambda b, p, v: (p, 0, 0)),
        ],
        out_specs=pass1,
        out_shape=jax.ShapeDtypeStruct((bsz, s, HG_WIDTH), BF16),
        scratch_shapes=[
            pltpu.VMEM((ROW_TILE, HG_WIDTH), F32),
            pltpu.VMEM((HG_HEADS, HG_DV, HG_DK), F32),
            pltpu.VMEM((s, HG_WIDTH), F32),
        ],
        compiler_params=_cparams("parallel", "arbitrary", "arbitrary"),
        name="hgrn2_bidir",
    )(qs, lf, kk, vals, gate, gnorm.reshape(1, -1), tri)


def _ssd_kernel(u_ref, dt_ref, z_ref, dtb_ref, alog_ref, dexp_ref, nw_ref, tril_ref, triu_ref,
                exp_ref, o_ref, st_s, acc_s, *, nct, nt):
    p = pl.program_id(1)
    v = pl.program_id(2)
    row0 = pl.multiple_of(_visit_to_tile(p, v, nct, nt) * MB_CHUNK, MB_CHUNK)
    fwd = p == 1
    L = MB_CHUNK

    @pl.when(v == 0)
    def _():
        st_s[...] = jnp.zeros_like(st_s)

    xs16 = u_ref[0, :, :MB_INNER]
    xs = xs16.astype(F32)
    bm = u_ref[0, :, MB_INNER:MB_INNER + MB_BC_WIDTH]
    cm = u_ref[0, :, MB_INNER + MB_BC_WIDTH:]

    xdt = dt_ref[0] + dtb_ref[...]
    dt = jnp.maximum(xdt, 0.0) + jnp.log(1.0 + jnp.exp(-jnp.abs(xdt)))
    dta = dt * (-jnp.exp(alog_ref[...]))
    cum_f = _dot2(tril_ref[...], dta)
    cum_b = _dot2(triu_ref[...], dta)
    cum = jnp.where(fwd, cum_f, cum_b)
    cum_end = jnp.where(fwd, cum[L - 1:L, :], cum[0:1, :])
    expand = exp_ref[0]
    e_in = _dot2_right(jnp.exp(cum), expand)
    e_out = _dot2_right(jnp.exp(cum_end - cum) * dt, expand)
    e_end = jnp.where(fwd, e_in[L - 1:L, :], e_in[0:1, :])
    x_out = (xs * e_out).astype(BF16)

    y_parts = []
    for g in range(MB_GROUPS):
        gsl = slice(g * MB_GROUP_WIDTH, (g + 1) * MB_GROUP_WIDTH)
        nsl = slice(g * MB_STATE, (g + 1) * MB_STATE)
        st = st_s[:, gsl]
        y_parts.append(_dot(cm[:, nsl], st.astype(BF16)) * e_in[:, gsl])
        st_s[:, gsl] = st * e_end[:, gsl] + _dot_tn(bm[:, nsl], x_out[:, gsl])
    y_inter = jnp.concatenate(y_parts, axis=1)

    rows = pl.ds(row0, L)

    @pl.when(p == 0)
    def _():
        acc_s[rows, :] = y_inter

    @pl.when(p == 1)
    def _():
        trow = lax.broadcasted_iota(jnp.int32, (L, L), 0)
        tcol = lax.broadcasted_iota(jnp.int32, (L, L), 1)
        lower = tcol <= trow
        diagonal = tcol == trow
        lane = lax.broadcasted_iota(jnp.int32, (L, LANES), 1)
        first_half = lane < MB_HEAD_DIM
        log_dt = jnp.log(dt)
        col_f = cum_f * LOG2E
        col_b = cum_b * LOG2E
        row_f_t = ((cum_f - log_dt) * LOG2E).T
        row_b_t = ((cum_b - log_dt) * LOG2E).T
        dt_t = dt.T
        pieces = []
        for g in range(MB_GROUPS):
            nsl = slice(g * MB_STATE, (g + 1) * MB_STATE)
            cb = _dot_nt(cm[:, nsl], bm[:, nsl])
            for hp in range(MB_HEADS_PER_GROUP // 2):
                ys = []
                for k in range(2):
                    h = g * MB_HEADS_PER_GROUP + 2 * hp + k
                    hb = MB_HEADS + h
                    seg_f = col_f[:, h:h + 1] - row_f_t[h:h + 1, :]
                    seg_b = col_b[:, hb:hb + 1] - row_b_t[hb:hb + 1, :]
                    e = jnp.exp2(jnp.where(lower, seg_f, seg_b))
                    m = (cb * (e + jnp.where(diagonal, dt_t[hb:hb + 1, :], 0.0))).astype(BF16)
                    c0 = (h // 2) * LANES
                    ys.append(_dot(m, xs16[:, c0:c0 + LANES]))
                pieces.append(jnp.where(first_half, ys[0], ys[1]))
        y = jnp.concatenate(pieces, axis=1) + y_inter + acc_s[rows, :]
        y = (y + dexp_ref[...] * xs) * z_ref[0].astype(F32)
        for g in range(MB_GROUPS):
            gsl = slice(g * MB_GROUP_WIDTH, (g + 1) * MB_GROUP_WIDTH)
            yg = y[:, gsl]
            ms = jnp.mean(yg * yg, axis=-1, keepdims=True)
            o_ref[0, :, gsl] = (yg * lax.rsqrt(ms + RMS_EPS) * nw_ref[:, gsl]).astype(o_ref.dtype)


def _ssd(u, dt_raw, z, dt_bias_row, a_log_row, d_exp, norm_w, tril, triu, expand, nct):
    bsz, s, _ = u.shape
    nt = s // MB_CHUNK
    tile = lambda p, v: _visit_to_tile(p, v, nct, nt)
    return pl.pallas_call(
        functools.partial(_ssd_kernel, nct=nct, nt=nt),
        grid=(bsz, 2, nt),
        in_specs=[
            pl.BlockSpec((1, MB_CHUNK, MB_CONV_DIM), lambda b, p, v: (b, tile(p, v), 0)),
            pl.BlockSpec((1, MB_CHUNK, LANES), lambda b, p, v: (b, tile(p, v), 0)),
            pl.BlockSpec((1, MB_CHUNK, MB_INNER), lambda b, p, v: (b, tile(p, v) * p, 0)),
            pl.BlockSpec((1, LANES), lambda b, p, v: (0, 0)),
            pl.BlockSpec((1, LANES), lambda b, p, v: (0, 0)),
            pl.BlockSpec((1, MB_INNER), lambda b, p, v: (0, 0)),
            pl.BlockSpec((1, MB_INNER), lambda b, p, v: (0, 0)),
            pl.BlockSpec((MB_CHUNK, MB_CHUNK), lambda b, p, v: (0, 0)),
            pl.BlockSpec((MB_CHUNK, MB_CHUNK), lambda b, p, v: (0, 0)),
            pl.BlockSpec((1, LANES, MB_INNER), lambda b, p, v: (p, 0, 0)),
        ],
        out_specs=pl.BlockSpec((1, MB_CHUNK, MB_INNER), lambda b, p, v: (b, tile(p, v) * p, 0)),
        out_shape=jax.ShapeDtypeStruct((bsz, s, MB_INNER), BF16),
        scratch_shapes=[
            pltpu.VMEM((MB_STATE, MB_INNER), F32),
            pltpu.VMEM((s, MB_INNER), F32),
        ],
        compiler_params=_cparams("parallel", "arbitrary", "arbitrary"),
        name="ssd_bidir",
    )(u, dt_raw, z, dt_bias_row, a_log_row, d_exp, norm_w, tril, triu, expand)


def _merge_kernel(x_ref, ohg_ref, oat_ref, omb_ref, gt_ref, g1_ref, whg_ref, wat_ref, wmb_ref,
                  wout_ref, lng_ref, lnb_ref, o_ref, *, alpha):
    gates = gt_ref[0].astype(F32)
    y = (gates[:, :D_MODEL] * _dot(ohg_ref[0], whg_ref[...])
         + gates[:, D_MODEL:2 * D_MODEL] * _dot(oat_ref[0], wat_ref[...])
         + gates[:, 2 * D_MODEL:] * _dot(omb_ref[0], wmb_ref[...]))
    y = _dot(y.astype(BF16), wout_ref[...])
    o_ref[0] = _layer_norm(alpha * x_ref[0] + g1_ref[0] * y, lng_ref[...], lnb_ref[...])


def _merge(xs, o_hg, o_at, o_mb, gates, modsel, w_hg, w_at, w_mb, w_out, ln_g, ln_b, nct, alpha):
    bsz, s, _ = xs.shape
    row = lambda width: pl.BlockSpec((1, ROW_TILE, width), lambda b, i: (b, i, 0))
    return pl.pallas_call(
        functools.partial(_merge_kernel, alpha=alpha),
        grid=(bsz, s // ROW_TILE),
        in_specs=[row(D_MODEL), row(HG_WIDTH), row(AT_WIDTH), row(MB_INNER), row(3 * D_MODEL),
                  _mod_spec(2, nct),
                  _resident(w_hg.shape), _resident(w_at.shape), _resident(w_mb.shape),
                  _resident(w_out.shape), _resident((1, D_MODEL)), _resident((1, D_MODEL))],
        out_specs=row(D_MODEL),
        out_shape=jax.ShapeDtypeStruct(xs.shape, F32),
        compiler_params=_cparams("parallel", "arbitrary"),
        name="merge_out_ln",
    )(xs, o_hg, o_at, o_mb, gates, modsel, w_hg, w_at, w_mb, w_out,
      ln_g.reshape(1, -1), ln_b.reshape(1, -1))


def _ffn_kernel(x_ref, sc_ref, sh_ref, g2_ref, win_ref, wout_ref, lng_ref, lnb_ref, o_ref, *, alpha):
    x = x_ref[0]
    h = x * (1.0 + sc_ref[0]) + sh_ref[0]
    gu = _dot(h.astype(BF16), win_ref[...])
    a = _silu(gu[:, :FFN_HIDDEN]) * gu[:, FFN_HIDDEN:]
    y = _dot(a.astype(BF16), wout_ref[...])
    o_ref[0] = _layer_norm(alpha * x + g2_ref[0] * y, lng_ref[...], lnb_ref[...])


def _ffn(xs, modsel, w_in, w_out, ln_g, ln_b, nct, alpha):
    bsz, s, _ = xs.shape
    row = pl.BlockSpec((1, ROW_TILE, D_MODEL), lambda b, i: (b, i, 0))
    return pl.pallas_call(
        functools.partial(_ffn_kernel, alpha=alpha),
        grid=(bsz, s // ROW_TILE),
        in_specs=[row, _mod_spec(4, nct), _mod_spec(3, nct), _mod_spec(5, nct),
                  _resident(w_in.shape), _resident(w_out.shape),
                  _resident((1, D_MODEL)), _resident((1, D_MODEL))],
        out_specs=row,
        out_shape=jax.ShapeDtypeStruct(xs.shape, F32),
        compiler_params=_cparams("parallel", "arbitrary"),
        name="swiglu_ffn_ln",
    )(xs, modsel, modsel, modsel, w_in, w_out, ln_g.reshape(1, -1), ln_b.reshape(1, -1))


def _rope_tables(seq, ctx_len):
    rows = seq // GRID_W
    row, col = jnp.meshgrid(jnp.arange(rows, dtype=F32), jnp.arange(GRID_W, dtype=F32), indexing="ij")
    n_pairs = AT_HEAD_DIM // 4
    inv_freq = ROPE_THETA ** (-jnp.arange(n_pairs, dtype=F32) / n_pairs)
    ang = jnp.concatenate([row.reshape(-1, 1) * inv_freq, col.reshape(-1, 1) * inv_freq], axis=-1)
    cos = jnp.repeat(jnp.cos(ang), 2, axis=-1)
    sin = jnp.repeat(jnp.sin(ang), 2, axis=-1) * jnp.tile(jnp.array([-1.0, 1.0], F32), AT_HEAD_DIM // 2)
    cos = jnp.concatenate([jnp.ones((ctx_len, AT_HEAD_DIM), F32), cos], axis=0)
    sin = jnp.concatenate([jnp.zeros((ctx_len, AT_HEAD_DIM), F32), sin], axis=0)
    return cos, sin


def _block_tri(n, blk):
    r = jnp.arange(n)
    same = (r[:, None] // blk) == (r[None, :] // blk)
    lower = (same & (r[None, :] <= r[:, None])).astype(BF16)
    upper = (same & (r[None, :] >= r[:, None])).astype(BF16)
    return lower, upper


def _head_expand():
    lane = jnp.arange(LANES)[:, None]
    head = jnp.arange(MB_INNER)[None, :] // MB_HEAD_DIM
    fwd = (lane == head).astype(BF16)
    bwd = (lane == head + MB_HEADS).astype(BF16)
    return jnp.stack([bwd, fwd])


def _pad_lanes(row):
    return jnp.pad(row.reshape(1, -1), ((0, 0), (0, LANES - row.size)))


def kernel(x, c, ctx, c_ctx, w_mod, b_mod, w_in, hg_lb, hg_gnorm, at_qnorm, at_knorm,
           mb_conv_w, mb_conv_b, mb_dt_bias, mb_a_log, mb_d, mb_norm,
           w_br_hg, w_br_at, w_br_mb, w_out, ln1_g, ln1_b, w_ffn_in, w_ffn_out, ln2_g, ln2_b):
    bsz, seq, _ = x.shape
    ctx_len = ctx.shape[1]
    depth = w_mod.shape[0]
    assert seq % ROW_TILE == 0 and ctx_len % ROW_TILE == 0 and seq % GRID_W == 0
    assert bsz + 1 <= MODS_ROWS
    alpha = (2 * depth) ** 0.25
    nct = ctx_len // ROW_TILE
    nct_mb = ctx_len // MB_CHUNK

    cos, sin = _rope_tables(seq, ctx_len)
    hg_tril, hg_triu = _block_tri(ROW_TILE, HG_SUB)
    hg_tri = jnp.stack([hg_triu, hg_tril])
    mb_tril, mb_triu = _block_tri(MB_CHUNK, MB_CHUNK)
    expand = _head_expand()

    lbs = _lower_bounds(hg_lb)
    cond = jnp.concatenate([c, c_ctx[None, :], jnp.zeros((MODS_ROWS - bsz - 1, D_MODEL), F32)], axis=0)
    mods = _modulations(cond, w_mod.astype(BF16), b_mod)
    ctx_rows = jnp.broadcast_to(mods[:, bsz:bsz + 1, :], (depth, bsz, mods.shape[-1]))
    modsel_all = jnp.stack([ctx_rows, mods[:, :bsz, :]], axis=2).reshape(depth, 2 * bsz, 1, -1)

    splits = [0]
    for width in IN_SIZES:
        splits.append(splits[-1] + width)
    o_hg, o_at, o_z, o_xbc, o_dt, o_gt = splits[0], splits[5], splits[8], splits[9], splits[10], splits[11]

    xs = jnp.concatenate([ctx, x], axis=1)
    for l in range(depth):
        modsel = modsel_all[l]
        w_l = w_in[l].astype(BF16)
        w_dt_in = jnp.pad(w_l[:, o_dt:o_gt], ((0, 0), (0, LANES - (o_gt - o_dt))))

        qs, lf, kk, vals, hg_gate = _inproj_hg(xs, modsel, w_l[:, o_hg:o_at], lbs[l], nct)
        q, k, v, z_act, gt_act, dt_raw, u = _inproj_rest(
            xs, modsel, w_l[:, o_at:o_z], w_l[:, o_z:o_xbc], w_l[:, o_gt:], w_dt_in, w_l[:, o_xbc:o_dt],
            cos, sin, at_qnorm[l], at_knorm[l], mb_conv_w[l], mb_conv_b[l], nct)

        out_hg = _hgrn2(qs, lf, kk, vals, hg_gate, hg_gnorm[l], hg_tri, nct)
        out_at = _attention(q, k, v, nct, ctx_len)
        out_mb = _ssd(u, dt_raw, z_act, _pad_lanes(mb_dt_bias[l]), _pad_lanes(mb_a_log[l]),
                      jnp.repeat(mb_d[l], MB_HEAD_DIM).reshape(1, -1), mb_norm[l].reshape(1, -1),
                      mb_tril, mb_triu, expand, nct_mb)

        xs = _merge(xs, out_hg, out_at, out_mb, gt_act, modsel,
                    w_br_hg[l].astype(BF16), w_br_at[l].astype(BF16), w_br_mb[l].astype(BF16),
                    w_out[l].astype(BF16), ln1_g[l], ln1_b[l], nct, alpha)
        xs = _ffn(xs, modsel, w_ffn_in[l].astype(BF16), w_ffn_out[l].astype(BF16),
                  ln2_g[l], ln2_b[l], nct, alpha)
    return xs[:, ctx_len:, :]
```

```python
import functools
import math

import jax
import jax.numpy as jnp
from jax import lax
from jax.experimental import pallas as pl
from jax.experimental.pallas import tpu as pltpu

F32 = jnp.float32
BF16 = jnp.bfloat16

D_MODEL = 1024
HG_HEADS = 8
HG_DK = 128
HG_DV = 128
HG_WIDTH = HG_HEADS * HG_DK
AT_HEADS = 8
AT_KV_HEADS = 2
AT_GROUP = AT_HEADS // AT_KV_HEADS
AT_HEAD_DIM = 128
AT_WIDTH = AT_HEADS * AT_HEAD_DIM
AT_KV_WIDTH = AT_KV_HEADS * AT_HEAD_DIM
GRID_W = 64
ROPE_THETA = 10000.0
MB_INNER = 2 * D_MODEL
MB_HEAD_DIM = 64
MB_HEADS = MB_INNER // MB_HEAD_DIM
MB_STATE = 128
MB_GROUPS = 4
MB_HEADS_PER_GROUP = MB_HEADS // MB_GROUPS
MB_CONV = 5
MB_BC_WIDTH = MB_GROUPS * MB_STATE
MB_CONV_DIM = MB_INNER + 2 * MB_BC_WIDTH
MB_GROUP_WIDTH = MB_INNER // MB_GROUPS
FFN_HIDDEN = ((8 * D_MODEL + 3 * 256 - 1) // (3 * 256)) * 256
IN_SIZES = (HG_WIDTH,) * 5 + (AT_WIDTH, AT_KV_WIDTH, AT_KV_WIDTH) + (
    MB_INNER, MB_CONV_DIM, 2 * MB_HEADS) + (3 * D_MODEL,)
LN_EPS = 1e-5
RMS_EPS = 1e-6
LOG2E = 1.4426950408889634

LANES = 128
SUBLANES = 8
VMEM_LIMIT_BYTES = 56 * 1024 * 1024

ROW_TILE = 256
HG_CHUNK = 64
HG_SUB = 32
MB_CHUNK = 128
MODS_ROWS = 24
CONV_CH_TILE = 512
CONV_STRIDE = (ROW_TILE + SUBLANES) // SUBLANES


def _cparams(*sem):
    return pltpu.CompilerParams(dimension_semantics=sem, vmem_limit_bytes=VMEM_LIMIT_BYTES)


def _resident(shape):
    nd = len(shape)
    return pl.BlockSpec(shape, lambda *_: (0,) * nd, pipeline_mode=pl.Buffered(1))


def _sigmoid(x):
    return 1.0 / (1.0 + jnp.exp(-x))


def _silu(x):
    return x * _sigmoid(x)


def _split_bf16(x):
    hi = x.astype(BF16)
    lo = (x - hi.astype(F32)).astype(BF16)
    return hi, lo


def _dot(a, b):
    return jnp.dot(a, b, preferred_element_type=F32)


def _dot_nt(a, b):
    return lax.dot_general(a, b, (((1,), (1,)), ((), ())), preferred_element_type=F32)


def _dot_tn(a, b):
    return lax.dot_general(a, b, (((0,), (0,)), ((), ())), preferred_element_type=F32)


def _dot2(m_bf16, x):
    hi, lo = _split_bf16(x)
    return _dot(m_bf16, hi) + _dot(m_bf16, lo)


def _dot2_right(x, m_bf16):
    hi, lo = _split_bf16(x)
    return _dot(hi, m_bf16) + _dot(lo, m_bf16)


def _layer_norm(v, g, b):
    mu = jnp.mean(v, axis=-1, keepdims=True)
    c = v - mu
    var = jnp.mean(c * c, axis=-1, keepdims=True)
    return c * lax.rsqrt(var + LN_EPS) * g + b


def _lbs_kernel(lb_ref, o_ref, *, depth):
    x = lb_ref[...]
    m = jnp.max(x, axis=0, keepdims=True)
    e = jnp.exp(x - m)
    p = e / jnp.sum(e, axis=0, keepdims=True)
    acc = p[0:1]
    for l in range(depth):
        if l > 0:
            acc = acc + p[l:l + 1]
        o_ref[l:l + 1, :] = acc - p[0:1]


def _lower_bounds(hg_lb):
    depth = hg_lb.shape[0]
    flat = hg_lb.reshape(depth, 2 * HG_WIDTH)
    out = pl.pallas_call(
        functools.partial(_lbs_kernel, depth=depth),
        out_shape=jax.ShapeDtypeStruct(flat.shape, F32),
        name="hg_lower_bounds",
    )(flat)
    return out.reshape(depth, 2, 1, HG_WIDTH)


def _mods_kernel(c_ref, w_ref, b_ref, o_ref):
    s = _silu(c_ref[...])
    o_ref[0] = _dot(s.astype(BF16), w_ref[0]) + b_ref[0]


def _modulations(cond, w_mod, b_mod):
    depth = w_mod.shape[0]
    nblk = w_mod.shape[2] // D_MODEL
    return pl.pallas_call(
        _mods_kernel,
        grid=(depth, nblk),
        in_specs=[
            pl.BlockSpec((MODS_ROWS, D_MODEL), lambda l, j: (0, 0)),
            pl.BlockSpec((1, D_MODEL, D_MODEL), lambda l, j: (l, 0, j)),
            pl.BlockSpec((1, 1, D_MODEL), lambda l, j: (l, 0, j)),
        ],
        out_specs=pl.BlockSpec((1, MODS_ROWS, D_MODEL), lambda l, j: (l, 0, j)),
        out_shape=jax.ShapeDtypeStruct((depth, MODS_ROWS, nblk * D_MODEL), F32),
        compiler_params=_cparams("arbitrary", "arbitrary"),
        name="adaln_modulations",
    )(cond, w_mod, b_mod.reshape(depth, 1, -1))


def _mod_spec(chunk, nct):
    return pl.BlockSpec((1, 1, D_MODEL),
                        lambda b, i: (2 * b + jnp.where(i >= nct, 1, 0), 0, chunk))


def _row_spec(width):
    return pl.BlockSpec((1, ROW_TILE, width), lambda b, i: (b, i, 0))


def _inproj_hg_kernel(x_ref, sc_ref, sh_ref, w_ref, lb_ref, q_ref, lf_ref, kk_ref, v_ref, g_ref):
    h = (x_ref[0] * (1.0 + sc_ref[0]) + sh_ref[0]).astype(BF16)
    W = HG_WIDTH

    def proj(j):
        return _dot(h, w_ref[:, j * W:(j + 1) * W])

    q_ref[0] = (_silu(proj(0)) * (HG_DK ** -0.5)).astype(BF16)
    for d in range(2):
        lb = lb_ref[d]
        xf = proj(1 + d)
        e = jnp.exp(-jnp.abs(xf))
        r = 1.0 / (1.0 + e)
        sig_pos = jnp.where(xf >= 0, r, e * r)
        sig_neg = jnp.where(xf >= 0, e * r, r)
        lf_ref[0, :, d * W:(d + 1) * W] = jnp.log(lb + (1.0 - lb) * sig_pos)
        kk_ref[0, :, d * W:(d + 1) * W] = ((1.0 - lb) * sig_neg).astype(BF16)
    v_ref[0] = proj(3).astype(BF16)
    g_ref[0] = _silu(proj(4)).astype(BF16)


def _inproj_hg(xs, modsel, w, lb, nct):
    bsz, s, _ = xs.shape
    shapes = [(HG_WIDTH, BF16), (2 * HG_WIDTH, F32), (2 * HG_WIDTH, BF16), (HG_WIDTH, BF16), (HG_WIDTH, BF16)]
    return pl.pallas_call(
        _inproj_hg_kernel,
        grid=(bsz, s // ROW_TILE),
        in_specs=[_row_spec(D_MODEL), _mod_spec(1, nct), _mod_spec(0, nct),
                  _resident(w.shape), _resident(lb.shape)],
        out_specs=[_row_spec(width) for width, _ in shapes],
        out_shape=[jax.ShapeDtypeStruct((bsz, s, width), dt) for width, dt in shapes],
        compiler_params=_cparams("parallel", "arbitrary"),
        name="inproj_hgrn2",
    )(xs, modsel, modsel, w, lb)


def _inproj_rest_kernel(x_ref, xp_ref, xn_ref, sc_ref, sh_ref, wat_ref, wz_ref, wgt_ref, wdt_ref, wx_ref,
                        cos_ref, sin_ref, qn_ref, kn_ref, cw_ref, cb_ref,
                        q_ref, k_ref, v_ref, z_ref, gt_ref, dt_ref, u_ref, pe_s, cv_s, *, nct, nt):
    i = pl.program_id(1)
    scale1 = 1.0 + sc_ref[0]
    shift = sh_ref[0]
    xm = x_ref[0] * scale1 + shift
    h = xm.astype(BF16)

    p = _dot(h, wat_ref[...])
    cos = cos_ref[...]
    sin = sin_ref[...]
    lane = lax.broadcasted_iota(jnp.int32, cos.shape, 1)
    even = (lane % 2) == 0

    def norm_rope(xh, w):
        ms = jnp.mean(xh * xh, axis=-1, keepdims=True)
        xn = xh * lax.rsqrt(ms + RMS_EPS) * w
        partner = jnp.where(even, pltpu.roll(xn, AT_HEAD_DIM - 1, 1), pltpu.roll(xn, 1, 1))
        return xn * cos + partner * sin

    qscale = AT_HEAD_DIM ** -0.5 * LOG2E
    for hh in range(AT_HEADS):
        sl = slice(hh * AT_HEAD_DIM, (hh + 1) * AT_HEAD_DIM)
        q_ref[0, :, sl] = (norm_rope(p[:, sl], qn_ref[...]) * qscale).astype(BF16)
    ones = jnp.ones((ROW_TILE, AT_HEAD_DIM), BF16)
    for kv in range(AT_KV_HEADS):
        sl = slice(kv * AT_HEAD_DIM, (kv + 1) * AT_HEAD_DIM)
        src = slice(AT_WIDTH + kv * AT_HEAD_DIM, AT_WIDTH + (kv + 1) * AT_HEAD_DIM)
        k_ref[0, :, sl] = norm_rope(p[:, src], kn_ref[...]).astype(BF16)
        vsrc = slice(AT_WIDTH + AT_KV_WIDTH + kv * AT_HEAD_DIM, AT_WIDTH + AT_KV_WIDTH + (kv + 1) * AT_HEAD_DIM)
        v_ref[0, :, 2 * kv * AT_HEAD_DIM:(2 * kv + 1) * AT_HEAD_DIM] = p[:, vsrc].astype(BF16)
        v_ref[0, :, (2 * kv + 1) * AT_HEAD_DIM:(2 * kv + 2) * AT_HEAD_DIM] = ones

    z_ref[0] = _silu(_dot(h, wz_ref[...])).astype(BF16)
    gt_ref[0] = _sigmoid(_dot(h, wgt_ref[...])).astype(BF16)
    dt_ref[0] = _dot(h, wdt_ref[...])

    half = MB_CONV // 2
    top = ROW_TILE + SUBLANES
    h_ext = jnp.concatenate([xp_ref[0] * scale1 + shift, xm, xn_ref[0] * scale1 + shift], axis=0).astype(BF16)
    prev_ok = jnp.where((i != 0) & (i != nct), 1.0, 0.0)
    next_ok = jnp.where((i != nct - 1) & (i != nt - 1), 1.0, 0.0)
    for c in range(MB_CONV_DIM // CONV_CH_TILE):
        res = _dot(h_ext, wx_ref[:, c * CONV_CH_TILE:(c + 1) * CONV_CH_TILE])
        for j in range(CONV_CH_TILE // LANES):
            lo = c * CONV_CH_TILE + j * LANES
            pe = pe_s.at[j]
            cv = cv_s.at[j]
            pe[0:SUBLANES, :] = res[0:SUBLANES, j * LANES:(j + 1) * LANES] * prev_ok
            pe[SUBLANES:top, :] = res[SUBLANES:top, j * LANES:(j + 1) * LANES]
            pe[top:top + SUBLANES, :] = res[top:top + SUBLANES, j * LANES:(j + 1) * LANES] * next_ok
            w = [jnp.broadcast_to(cw_ref[t:t + 1, lo:lo + LANES], (SUBLANES, LANES)) for t in range(MB_CONV)]
            bias = jnp.broadcast_to(cb_ref[:, lo:lo + LANES], (SUBLANES, LANES))
            taps = [pe[pl.ds(k, SUBLANES, stride=CONV_STRIDE), :] for k in range(CONV_STRIDE + MB_CONV - 1)]
            for k in range(CONV_STRIDE):
                acc = bias + w[0] * taps[k]
                for t in range(1, MB_CONV):
                    acc = acc + w[t] * taps[k + t]
                cv[pl.ds(k + half, SUBLANES, stride=CONV_STRIDE), :] = _silu(acc)
            u_ref[0, :, lo:lo + LANES] = cv[SUBLANES:top, :].astype(BF16)


def _inproj_rest(xs, modsel, w_at, w_z, w_gt, w_dt, w_xbc, cos, sin, qn, kn, conv_w, conv_b, nct):
    bsz, s, _ = xs.shape
    nt = s // ROW_TILE
    halo_blocks = ROW_TILE // SUBLANES
    last_halo = s // SUBLANES - 1
    shapes = [(AT_WIDTH, BF16), (AT_KV_WIDTH, BF16), (2 * AT_KV_WIDTH, BF16), (MB_INNER, BF16),
              (3 * D_MODEL, BF16), (LANES, F32), (MB_CONV_DIM, BF16)]
    return pl.pallas_call(
        functools.partial(_inproj_rest_kernel, nct=nct, nt=nt),
        grid=(bsz, nt),
        in_specs=[
            _row_spec(D_MODEL),
            pl.BlockSpec((1, SUBLANES, D_MODEL), lambda b, i: (b, jnp.maximum(i * halo_blocks - 1, 0), 0)),
            pl.BlockSpec((1, SUBLANES, D_MODEL), lambda b, i: (b, jnp.minimum((i + 1) * halo_blocks, last_halo), 0)),
            _mod_spec(1, nct), _mod_spec(0, nct),
            _resident(w_at.shape), _resident(w_z.shape), _resident(w_gt.shape), _resident(w_dt.shape),
            _resident(w_xbc.shape),
            pl.BlockSpec((ROW_TILE, AT_HEAD_DIM), lambda b, i: (i, 0)),
            pl.BlockSpec((ROW_TILE, AT_HEAD_DIM), lambda b, i: (i, 0)),
            _resident((1, AT_HEAD_DIM)), _resident((1, AT_HEAD_DIM)),
            _resident(conv_w.shape), _resident((1, MB_CONV_DIM)),
        ],
        out_specs=[_row_spec(width) for width, _ in shapes],
        out_shape=[jax.ShapeDtypeStruct((bsz, s, width), dt) for width, dt in shapes],
        scratch_shapes=[pltpu.VMEM((CONV_CH_TILE // LANES, ROW_TILE + 2 * SUBLANES, LANES), F32),
                        pltpu.VMEM((CONV_CH_TILE // LANES, ROW_TILE + 2 * SUBLANES, LANES), F32)],
        compiler_params=_cparams("parallel", "arbitrary"),
        name="inproj_attn_ssd_gates",
    )(xs, xs, xs, modsel, modsel, w_at, w_z, w_gt, w_dt, w_xbc, cos, sin,
      qn.reshape(1, -1), kn.reshape(1, -1), conv_w, conv_b.reshape(1, -1))


def _attn_kernel(q_ref, k_ref, v_ref, o_ref, *, nct, ctx_len):
    i = pl.program_id(2)

    def attend(keys, vals):
        for hh in range(AT_GROUP):
            sl = slice(hh * AT_HEAD_DIM, (hh + 1) * AT_HEAD_DIM)
            s = _dot_nt(q_ref[0, :, sl], keys)
            m = jnp.max(s, axis=-1, keepdims=True)
            e = jnp.exp2(s - m).astype(BF16)
            ol = _dot(e, vals)
            o_ref[0, :, sl] = (ol[:, :AT_HEAD_DIM] / ol[:, AT_HEAD_DIM:]).astype(o_ref.dtype)

    @pl.when(i < nct)
    def _():
        attend(k_ref[0, :ctx_len, :], v_ref[0, :ctx_len, :])

    @pl.when(i >= nct)
    def _():
        attend(k_ref[0], v_ref[0])


def _attention(q, k, v, nct, ctx_len):
    bsz, s, _ = q.shape
    gw = AT_GROUP * AT_HEAD_DIM
    k_spec = pl.BlockSpec((1, s, AT_HEAD_DIM), lambda b, g, i: (b, 0, g))
    v_spec = pl.BlockSpec((1, s, 2 * AT_HEAD_DIM), lambda b, g, i: (b, 0, g))
    return pl.pallas_call(
        functools.partial(_attn_kernel, nct=nct, ctx_len=ctx_len),
        grid=(bsz, AT_KV_HEADS, s // ROW_TILE),
        in_specs=[pl.BlockSpec((1, ROW_TILE, gw), lambda b, g, i: (b, i, g)), k_spec, v_spec],
        out_specs=pl.BlockSpec((1, ROW_TILE, gw), lambda b, g, i: (b, i, g)),
        out_shape=jax.ShapeDtypeStruct((bsz, s, AT_WIDTH), BF16),
        compiler_params=_cparams("parallel", "parallel", "arbitrary"),
        name="gqa_attention",
    )(q, k, v)


def _visit_to_tile(p, v, nct, nt):
    backward = jnp.where(v < nct, nct - 1 - v, nt - 1 - (v - nct))
    return jnp.where(p == 1, v, backward)


def _hgrn2_kernel(q_ref, lf_ref, kk_ref, v_ref, g_ref, gn_ref, tri_ref, o_ref,
                  b_s, st_s, acc_s, *, nct, nt):
    p = pl.program_id(1)
    v = pl.program_id(2)
    row0 = pl.multiple_of(_visit_to_tile(p, v, nct, nt) * ROW_TILE, ROW_TILE)

    @pl.when(v == 0)
    def _():
        st_s[...] = jnp.zeros_like(st_s)

    b_s[...] = _dot2(tri_ref[0], lf_ref[0])

    @pl.when(p == 0)
    def _():
        _hgrn2_tile(False, row0, q_ref, kk_ref, v_ref, b_s, st_s, acc_s)

    @pl.when(p == 1)
    def _():
        _hgrn2_tile(True, row0, q_ref, kk_ref, v_ref, b_s, st_s, acc_s)
        gate = g_ref[0].astype(F32)
        for h in range(HG_HEADS):
            sl = slice(h * HG_DV, (h + 1) * HG_DV)
            o = acc_s[pl.ds(row0, ROW_TILE), sl]
            ms = jnp.mean(o * o, axis=-1, keepdims=True)
            o_ref[0, :, sl] = (o * lax.rsqrt(ms + RMS_EPS) * gn_ref[...] * gate[:, sl]).astype(o_ref.dtype)


def _hgrn2_tile(fwd, row0, q_ref, kk_ref, v_ref, b_s, st_s, acc_s):
    C, SB, W = HG_CHUNK, HG_SUB, HG_WIDTH
    nb = C // SB
    n_chunks = ROW_TILE // C
    end_r = SB - 1 if fwd else 0
    mid_r = SB // 2 - 1 if fwd else SB // 2
    trow = lax.broadcasted_iota(jnp.int32, (C, C), 0)
    tcol = lax.broadcasted_iota(jnp.int32, (C, C), 1)
    blk_gap = (trow // SB - tcol // SB) if fwd else (tcol // SB - trow // SB)
    same_blk = (blk_gap == 0) & ((tcol <= trow) if fwd else (tcol >= trow))
    order = list(range(nb)) if fwd else list(reversed(range(nb)))

    def per_block(rows):
        return jnp.concatenate([jnp.broadcast_to(x, (SB, W)) for x in rows], axis=0)

    zero_row = jnp.zeros((1, W), F32)
    for c in (range(n_chunks) if fwd else reversed(range(n_chunks))):
        r0 = c * C
        u = b_s[r0:r0 + C, :]
        d = [b_s[r0 + R * SB + end_r:r0 + R * SB + end_r + 1, :] for R in range(nb)]
        mid = per_block([b_s[r0 + R * SB + mid_r:r0 + R * SB + mid_r + 1, :] for R in range(nb)])
        d_full = per_block(d)
        before = {}
        run = zero_row
        for R in order:
            before[R] = run
            run = run + d[R]
        total = run
        e_before = per_block([jnp.exp(before[R]) for R in range(nb)])
        e_after = per_block([jnp.exp(total - before[R] - d[R]) for R in range(nb)])

        qq = q_ref[0, r0:r0 + C, :].astype(F32)
        kk = kk_ref[0, r0:r0 + C, :].astype(F32)
        q_in = (qq * jnp.exp(u - mid)).astype(BF16)
        k_in = (kk * jnp.exp(mid - u)).astype(BF16)
        q_blk = qq * jnp.exp(u)
        k_blk = kk * jnp.exp(d_full - u)
        k_blk16 = k_blk.astype(BF16)
        q_state = (q_blk * e_before).astype(BF16)
        k_state = (k_blk * e_after).astype(BF16)
        decay = jnp.exp(total)
        q_gap = {1: q_blk.astype(BF16)}
        for gap in range(2, nb):
            between = []
            for R in range(nb):
                J = R - gap if fwd else R + gap
                if 0 <= J < nb:
                    mids = range(J + 1, R) if fwd else range(R + 1, J)
                    g = zero_row
                    for m in mids:
                        g = g + d[m]
                    between.append(g)
                else:
                    between.append(zero_row)
            q_gap[gap] = (q_blk * per_block([jnp.exp(g) for g in between])).astype(BF16)
        vv = v_ref[0, r0:r0 + C, :]

        outs, states = [], []
        for h in range(HG_HEADS):
            sl = slice(h * HG_DK, (h + 1) * HG_DK)
            a = jnp.where(same_blk, _dot_nt(q_in[:, sl], k_in[:, sl]), 0.0)
            for gap in range(1, nb):
                a = a + jnp.where(blk_gap == gap, _dot_nt(q_gap[gap][:, sl], k_blk16[:, sl]), 0.0)
            st = st_s[h]
            outs.append(_dot(a.astype(BF16), vv[:, sl]) + _dot_nt(q_state[:, sl], st.astype(BF16)))
            states.append(st * decay[:, sl] + _dot_tn(vv[:, sl], k_state[:, sl]))
        for h in range(HG_HEADS):
            st_s[h] = states[h]
        o = jnp.concatenate(outs, axis=1)
        acc_rows = pl.ds(row0 + r0, C)
        if fwd:
            acc_s[acc_rows, :] = acc_s[acc_rows, :] + o
        else:
            acc_s[acc_rows, :] = o


def _hgrn2(qs, lf, kk, vals, gate, gnorm, tri, nct):
    bsz, s, _ = qs.shape
    nt = s // ROW_TILE
    tile = lambda p, v: _visit_to_tile(p, v, nct, nt)
    both = pl.BlockSpec((1, ROW_TILE, HG_WIDTH), lambda b, p, v: (b, tile(p, v), 0))
    by_dir = pl.BlockSpec((1, ROW_TILE, HG_WIDTH), lambda b, p, v: (b, tile(p, v), 1 - p))
    pass1 = pl.BlockSpec((1, ROW_TILE, HG_WIDTH), lambda b, p, v: (b, tile(p, v) * p, 0))
    return pl.pallas_call(
        functools.partial(_hgrn2_kernel, nct=nct, nt=nt),
        grid=(bsz, 2, nt),
        in_specs=[
            both, by_dir, by_dir, both, pass1,
            pl.BlockSpec((1, HG_DV), lambda b, p, v: (0, 0)),
            pl.BlockSpec((1, ROW_TILE, ROW_TILE), lambda b, p, v: (p, 0, 0)),
        ],
        out_specs=pass1,
        out_shape=jax.ShapeDtypeStruct((bsz, s, HG_WIDTH), BF16),
        scratch_shapes=[
            pltpu.VMEM((ROW_TILE, HG_WIDTH), F32),
            pltpu.VMEM((HG_HEADS, HG_DV, HG_DK), F32),
            pltpu.VMEM((s, HG_WIDTH), F32),
        ],
        compiler_params=_cparams("parallel", "arbitrary", "arbitrary"),
        name="hgrn2_bidir",
    )(qs, lf, kk, vals, gate, gnorm.reshape(1, -1), tri)


def _ssd_kernel(u_ref, dt_ref, z_ref, dtb_ref, alog_ref, dexp_ref, nw_ref, tril_ref, triu_ref,
                exp_ref, o_ref, st_s, acc_s, *, nct, nt):
    p = pl.program_id(1)
    v = pl.program_id(2)
    row0 = pl.multiple_of(_visit_to_tile(p, v, nct, nt) * MB_CHUNK, MB_CHUNK)
    fwd = p == 1
    L = MB_CHUNK

    @pl.when(v == 0)
    def _():
        st_s[...] = jnp.zeros_like(st_s)

    xs16 = u_ref[0, :, :MB_INNER]
    xs = xs16.astype(F32)
    bm = u_ref[0, :, MB_INNER:MB_INNER + MB_BC_WIDTH]
    cm = u_ref[0, :, MB_INNER + MB_BC_WIDTH:]

    xdt = dt_ref[0] + dtb_ref[...]
    dt = jnp.maximum(xdt, 0.0) + jnp.log(1.0 + jnp.exp(-jnp.abs(xdt)))
    dta = dt * (-jnp.exp(alog_ref[...]))
    cum_f = _dot2(tril_ref[...], dta)
    cum_b = _dot2(triu_ref[...], dta)
    cum = jnp.where(fwd, cum_f, cum_b)
    cum_end = jnp.where(fwd, cum[L - 1:L, :], cum[0:1, :])
    expand = exp_ref[0]
    e_in = _dot(jnp.exp(cum).astype(BF16), expand)
    e_out = _dot((jnp.exp(cum_end - cum) * dt).astype(BF16), expand)
    e_end = jnp.where(fwd, e_in[L - 1:L, :], e_in[0:1, :])
    x_out = (xs * e_out).astype(BF16)

    y_parts = []
    for g in range(MB_GROUPS):
        gsl = slice(g * MB_GROUP_WIDTH, (g + 1) * MB_GROUP_WIDTH)
        nsl = slice(g * MB_STATE, (g + 1) * MB_STATE)
        st = st_s[:, gsl]
        y_parts.append(_dot(cm[:, nsl], st.astype(BF16)) * e_in[:, gsl])
        st_s[:, gsl] = st * e_end[:, gsl] + _dot_tn(bm[:, nsl], x_out[:, gsl])
    y_inter = jnp.concatenate(y_parts, axis=1)

    rows = pl.ds(row0, L)

    @pl.when(p == 0)
    def _():
        acc_s[rows, :] = y_inter

    @pl.when(p == 1)
    def _():
        trow = lax.broadcasted_iota(jnp.int32, (L, L), 0)
        tcol = lax.broadcasted_iota(jnp.int32, (L, L), 1)
        lower = tcol <= trow
        diagonal = tcol == trow
        lane = lax.broadcasted_iota(jnp.int32, (L, LANES), 1)
        first_half = lane < MB_HEAD_DIM
        log_dt = jnp.log(dt)
        col_f = cum_f * LOG2E
        col_b = cum_b * LOG2E
        row_f_t = ((cum_f - log_dt) * LOG2E).T
        row_b_t = ((cum_b - log_dt) * LOG2E).T
        dt_t = dt.T
        pieces = []
        for g in range(MB_GROUPS):
            nsl = slice(g * MB_STATE, (g + 1) * MB_STATE)
            cb = _dot_nt(cm[:, nsl], bm[:, nsl])
            for hp in range(MB_HEADS_PER_GROUP // 2):
                ys = []
                for k in range(2):
                    h = g * MB_HEADS_PER_GROUP + 2 * hp + k
                    hb = MB_HEADS + h
                    seg_f = col_f[:, h:h + 1] - row_f_t[h:h + 1, :]
                    seg_b = col_b[:, hb:hb + 1] - row_b_t[hb:hb + 1, :]
                    e = jnp.exp2(jnp.where(lower, seg_f, seg_b))
                    m = (cb * (e + jnp.where(diagonal, dt_t[hb:hb + 1, :], 0.0))).astype(BF16)
                    c0 = (h // 2) * LANES
                    ys.append(_dot(m, xs16[:, c0:c0 + LANES]))
                pieces.append(jnp.where(first_half, ys[0], ys[1]))
        y = jnp.concatenate(pieces, axis=1) + y_inter + acc_s[rows, :]
        y = (y + dexp_ref[...] * xs) * z_ref[0].astype(F32)
        for g in range(MB_GROUPS):
            gsl = slice(g * MB_GROUP_WIDTH, (g + 1) * MB_GROUP_WIDTH)
            yg = y[:, gsl]
            ms = jnp.mean(yg * yg, axis=-1, keepdims=True)
            o_ref[0, :, gsl] = (yg * lax.rsqrt(ms + RMS_EPS) * nw_ref[:, gsl]).astype(o_ref.dtype)


def _ssd(u, dt_raw, z, dt_bias_row, a_log_row, d_exp, norm_w, tril, triu, expand, nct):
    bsz, s, _ = u.shape
    nt = s // MB_CHUNK
    tile = lambda p, v: _visit_to_tile(p, v, nct, nt)
    return pl.pallas_call(
        functools.partial(_ssd_kernel, nct=nct, nt=nt),
        grid=(bsz, 2, nt),
        in_specs=[
            pl.BlockSpec((1, MB_CHUNK, MB_CONV_DIM), lambda b, p, v: (b, tile(p, v), 0)),
            pl.BlockSpec((1, MB_CHUNK, LANES), lambda b, p, v: (b, tile(p, v), 0)),
            pl.BlockSpec((1, MB_CHUNK, MB_INNER), lambda b, p, v: (b, tile(p, v) * p, 0)),
            pl.BlockSpec((1, LANES), lambda b, p, v: (0, 0)),
            pl.BlockSpec((1, LANES), lambda b, p, v: (0, 0)),
            pl.BlockSpec((1, MB_INNER), lambda b, p, v: (0, 0)),
            pl.BlockSpec((1, MB_INNER), lambda b, p, v: (0, 0)),
            pl.BlockSpec((MB_CHUNK, MB_CHUNK), lambda b, p, v: (0, 0)),
            pl.BlockSpec((MB_CHUNK, MB_CHUNK), lambda b, p, v: (0, 0)),
            pl.BlockSpec((1, LANES, MB_INNER), lambda b, p, v: (p, 0, 0)),
        ],
        out_specs=pl.BlockSpec((1, MB_CHUNK, MB_INNER), lambda b, p, v: (b, tile(p, v) * p, 0)),
        out_shape=jax.ShapeDtypeStruct((bsz, s, MB_INNER), BF16),
        scratch_shapes=[
            pltpu.VMEM((MB_STATE, MB_INNER), F32),
            pltpu.VMEM((s, MB_INNER), F32),
        ],
        compiler_params=_cparams("parallel", "arbitrary", "arbitrary"),
        name="ssd_bidir",
    )(u, dt_raw, z, dt_bias_row, a_log_row, d_exp, norm_w, tril, triu, expand)


def _merge_kernel(x_ref, ohg_ref, oat_ref, omb_ref, gt_ref, g1_ref, whg_ref, wat_ref, wmb_ref,
                  wout_ref, lng_ref, lnb_ref, o_ref, *, alpha):
    gates = gt_ref[0].astype(F32)
    y = (gates[:, :D_MODEL] * _dot(ohg_ref[0], whg_ref[...])
         + gates[:, D_MODEL:2 * D_MODEL] * _dot(oat_ref[0], wat_ref[...])
         + gates[:, 2 * D_MODEL:] * _dot(omb_ref[0], wmb_ref[...]))
    y = _dot(y.astype(BF16), wout_ref[...])
    o_ref[0] = _layer_norm(alpha * x_ref[0] + g1_ref[0] * y, lng_ref[...], lnb_ref[...])


def _merge(xs, o_hg, o_at, o_mb, gates, modsel, w_hg, w_at, w_mb, w_out, ln_g, ln_b, nct, alpha):
    bsz, s, _ = xs.shape
    row = lambda width: pl.BlockSpec((1, ROW_TILE, width), lambda b, i: (b, i, 0))
    return pl.pallas_call(
        functools.partial(_merge_kernel, alpha=alpha),
        grid=(bsz, s // ROW_TILE),
        in_specs=[row(D_MODEL), row(HG_WIDTH), row(AT_WIDTH), row(MB_INNER), row(3 * D_MODEL),
                  _mod_spec(2, nct),
                  _resident(w_hg.shape), _resident(w_at.shape), _resident(w_mb.shape),
                  _resident(w_out.shape), _resident((1, D_MODEL)), _resident((1, D_MODEL))],
        out_specs=row(D_MODEL),
        out_shape=jax.ShapeDtypeStruct(xs.shape, F32),
        compiler_params=_cparams("parallel", "arbitrary"),
        name="merge_out_ln",
    )(xs, o_hg, o_at, o_mb, gates, modsel, w_hg, w_at, w_mb, w_out,
      ln_g.reshape(1, -1), ln_b.reshape(1, -1))


def _ffn_kernel(x_ref, sc_ref, sh_ref, g2_ref, win_ref, wout_ref, lng_ref, lnb_ref, o_ref, *, alpha):
    x = x_ref[0]
    h = x * (1.0 + sc_ref[0]) + sh_ref[0]
    gu = _dot(h.astype(BF16), win_ref[...])
    a = _silu(gu[:, :FFN_HIDDEN]) * gu[:, FFN_HIDDEN:]
    y = _dot(a.astype(BF16), wout_ref[...])
    o_ref[0] = _layer_norm(alpha * x + g2_ref[0] * y, lng_ref[...], lnb_ref[...])


def _ffn(xs, modsel, w_in, w_out, ln_g, ln_b, nct, alpha):
    bsz, s, _ = xs.shape
    row = pl.BlockSpec((1, ROW_TILE, D_MODEL), lambda b, i: (b, i, 0))
    return pl.pallas_call(
        functools.partial(_ffn_kernel, alpha=alpha),
        grid=(bsz, s // ROW_TILE),
        in_specs=[row, _mod_spec(4, nct), _mod_spec(3, nct), _mod_spec(5, nct),
                  _resident(w_in.shape), _resident(w_out.shape),
                  _resident((1, D_MODEL)), _resident((1, D_MODEL))],
        out_specs=row,
        out_shape=jax.ShapeDtypeStruct(xs.shape, F32),
        compiler_params=_cparams("parallel", "arbitrary"),
        name="swiglu_ffn_ln",
    )(xs, modsel, modsel, modsel, w_in, w_out, ln_g.reshape(1, -1), ln_b.reshape(1, -1))


def _rope_tables(seq, ctx_len):
    rows = seq // GRID_W
    row, col = jnp.meshgrid(jnp.arange(rows, dtype=F32), jnp.arange(GRID_W, dtype=F32), indexing="ij")
    n_pairs = AT_HEAD_DIM // 4
    inv_freq = ROPE_THETA ** (-jnp.arange(n_pairs, dtype=F32) / n_pairs)
    ang = jnp.concatenate([row.reshape(-1, 1) * inv_freq, col.reshape(-1, 1) * inv_freq], axis=-1)
    cos = jnp.repeat(jnp.cos(ang), 2, axis=-1)
    sin = jnp.repeat(jnp.sin(ang), 2, axis=-1) * jnp.tile(jnp.array([-1.0, 1.0], F32), AT_HEAD_DIM // 2)
    cos = jnp.concatenate([jnp.ones((ctx_len, AT_HEAD_DIM), F32), cos], axis=0)
    sin = jnp.concatenate([jnp.zeros((ctx_len, AT_HEAD_DIM), F32), sin], axis=0)
    return cos, sin


def _block_tri(n, blk):
    r = jnp.arange(n)
    same = (r[:, None] // blk) == (r[None, :] // blk)
    lower = (same & (r[None, :] <= r[:, None])).astype(BF16)
    upper = (same & (r[None, :] >= r[:, None])).astype(BF16)
    return lower, upper


def _head_expand():
    lane = jnp.arange(LANES)[:, None]
    head = jnp.arange(MB_INNER)[None, :] // MB_HEAD_DIM
    fwd = (lane == head).astype(BF16)
    bwd = (lane == head + MB_HEADS).astype(BF16)
    return jnp.stack([bwd, fwd])


def _pad_lanes(row):
    return jnp.pad(row.reshape(1, -1), ((0, 0), (0, LANES - row.size)))


def kernel(x, c, ctx, c_ctx, w_mod, b_mod, w_in, hg_lb, hg_gnorm, at_qnorm, at_knorm,
           mb_conv_w, mb_conv_b, mb_dt_bias, mb_a_log, mb_d, mb_norm,
           w_br_hg, w_br_at, w_br_mb, w_out, ln1_g, ln1_b, w_ffn_in, w_ffn_out, ln2_g, ln2_b):
    bsz, seq, _ = x.shape
    ctx_len = ctx.shape[1]
    depth = w_mod.shape[0]
    assert seq % ROW_TILE == 0 and ctx_len % ROW_TILE == 0 and seq % GRID_W == 0
    assert bsz + 1 <= MODS_ROWS
    alpha = (2 * depth) ** 0.25
    nct = ctx_len // ROW_TILE
    nct_mb = ctx_len // MB_CHUNK

    cos, sin = _rope_tables(seq, ctx_len)
    hg_tril, hg_triu = _block_tri(ROW_TILE, HG_SUB)
    hg_tri = jnp.stack([hg_triu, hg_tril])
    mb_tril, mb_triu = _block_tri(MB_CHUNK, MB_CHUNK)
    expand = _head_expand()

    lbs = _lower_bounds(hg_lb)
    cond = jnp.concatenate([c, c_ctx[None, :], jnp.zeros((MODS_ROWS - bsz - 1, D_MODEL), F32)], axis=0)
    mods = _modulations(cond, w_mod.astype(BF16), b_mod)
    ctx_rows = jnp.broadcast_to(mods[:, bsz:bsz + 1, :], (depth, bsz, mods.shape[-1]))
    modsel_all = jnp.stack([ctx_rows, mods[:, :bsz, :]], axis=2).reshape(depth, 2 * bsz, 1, -1)

    splits = [0]
    for width in IN_SIZES:
        splits.append(splits[-1] + width)
    o_hg, o_at, o_z, o_xbc, o_dt, o_gt = splits[0], splits[5], splits[8], splits[9], splits[10], splits[11]

    xs = jnp.concatenate([ctx, x], axis=1)
    for l in range(depth):
        modsel = modsel_all[l]
        w_cols = lambda a, b: w_in[l, :, a:b].astype(BF16)
        w_dt_in = jnp.pad(w_cols(o_dt, o_gt), ((0, 0), (0, LANES - (o_gt - o_dt))))

        qs, lf, kk, vals, hg_gate = _inproj_hg(xs, modsel, w_cols(o_hg, o_at), lbs[l], nct)
        q, k, v, z_act, gt_act, dt_raw, u = _inproj_rest(
            xs, modsel, w_cols(o_at, o_z), w_cols(o_z, o_xbc), w_cols(o_gt, splits[-1]), w_dt_in,
            w_cols(o_xbc, o_dt), cos, sin, at_qnorm[l], at_knorm[l], mb_conv_w[l], mb_conv_b[l], nct)

        out_hg = _hgrn2(qs, lf, kk, vals, hg_gate, hg_gnorm[l], hg_tri, nct)
        out_at = _attention(q, k, v, nct, ctx_len)
        out_mb = _ssd(u, dt_raw, z_act, _pad_lanes(mb_dt_bias[l]), _pad_lanes(mb_a_log[l]),
                      jnp.repeat(mb_d[l], MB_HEAD_DIM).reshape(1, -1), mb_norm[l].reshape(1, -1),
                      mb_tril, mb_triu, expand, nct_mb)

        xs = _merge(xs, out_hg, out_at, out_mb, gt_act, modsel,
                    w_br_hg[l].astype(BF16), w_br_at[l].astype(BF16), w_br_mb[l].astype(BF16),
                    w_out[l].astype(BF16), ln1_g[l], ln1_b[l], nct, alpha)
        xs = _ffn(xs, modsel, w_ffn_in[l].astype(BF16), w_ffn_out[l].astype(BF16),
                  ln2_g[l], ln2_b[l], nct, alpha)
    return xs[:, ctx_len:, :]
```

```python
import functools
import math

import jax
import jax.numpy as jnp
from jax import lax
from jax.experimental import pallas as pl
from jax.experimental.pallas import tpu as pltpu

F32 = jnp.float32
BF16 = jnp.bfloat16

D_MODEL = 1024
HG_HEADS = 8
HG_DK = 128
HG_DV = 128
HG_WIDTH = HG_HEADS * HG_DK
AT_HEADS = 8
AT_KV_HEADS = 2
AT_GROUP = AT_HEADS // AT_KV_HEADS
AT_HEAD_DIM = 128
AT_WIDTH = AT_HEADS * AT_HEAD_DIM
AT_KV_WIDTH = AT_KV_HEADS * AT_HEAD_DIM
GRID_W = 64
ROPE_THETA = 10000.0
MB_INNER = 2 * D_MODEL
MB_HEAD_DIM = 64
MB_HEADS = MB_INNER // MB_HEAD_DIM
MB_STATE = 128
MB_GROUPS = 4
MB_HEADS_PER_GROUP = MB_HEADS // MB_GROUPS
MB_CONV = 5
MB_BC_WIDTH = MB_GROUPS * MB_STATE
MB_CONV_DIM = MB_INNER + 2 * MB_BC_WIDTH
MB_GROUP_WIDTH = MB_INNER // MB_GROUPS
FFN_HIDDEN = ((8 * D_MODEL + 3 * 256 - 1) // (3 * 256)) * 256
IN_SIZES = (HG_WIDTH,) * 5 + (AT_WIDTH, AT_KV_WIDTH, AT_KV_WIDTH) + (
    MB_INNER, MB_CONV_DIM, 2 * MB_HEADS) + (3 * D_MODEL,)
LN_EPS = 1e-5
RMS_EPS = 1e-6
LOG2E = 1.4426950408889634

LANES = 128
SUBLANES = 8
VMEM_LIMIT_BYTES = 56 * 1024 * 1024

ROW_TILE = 256
HG_CHUNK = 64
HG_SUB = 32
HG_BATCH = 2
MB_CHUNK = 128
MODS_ROWS = 24
CONV_CH_TILE = 512
CONV_STRIDE = (ROW_TILE + SUBLANES) // SUBLANES


def _cparams(*sem):
    return pltpu.CompilerParams(dimension_semantics=sem, vmem_limit_bytes=VMEM_LIMIT_BYTES)


def _resident(shape):
    nd = len(shape)
    return pl.BlockSpec(shape, lambda *_: (0,) * nd, pipeline_mode=pl.Buffered(1))


def _sigmoid(x):
    return 1.0 / (1.0 + jnp.exp(-x))


def _silu(x):
    return x * _sigmoid(x)


def _split_bf16(x):
    hi = x.astype(BF16)
    lo = (x - hi.astype(F32)).astype(BF16)
    return hi, lo


def _dot(a, b):
    return jnp.dot(a, b, preferred_element_type=F32)


def _dot_nt(a, b):
    return lax.dot_general(a, b, (((1,), (1,)), ((), ())), preferred_element_type=F32)


def _dot_tn(a, b):
    return lax.dot_general(a, b, (((0,), (0,)), ((), ())), preferred_element_type=F32)


def _dot2(m_bf16, x):
    hi, lo = _split_bf16(x)
    return _dot(m_bf16, hi) + _dot(m_bf16, lo)


def _dot2_right(x, m_bf16):
    hi, lo = _split_bf16(x)
    return _dot(hi, m_bf16) + _dot(lo, m_bf16)


def _layer_norm(v, g, b):
    mu = jnp.mean(v, axis=-1, keepdims=True)
    c = v - mu
    var = jnp.mean(c * c, axis=-1, keepdims=True)
    return c * lax.rsqrt(var + LN_EPS) * g + b


def _lbs_kernel(lb_ref, o_ref, *, depth):
    x = lb_ref[...]
    m = jnp.max(x, axis=0, keepdims=True)
    e = jnp.exp(x - m)
    p = e / jnp.sum(e, axis=0, keepdims=True)
    acc = p[0:1]
    for l in range(depth):
        if l > 0:
            acc = acc + p[l:l + 1]
        o_ref[l:l + 1, :] = acc - p[0:1]


def _lower_bounds(hg_lb):
    depth = hg_lb.shape[0]
    flat = hg_lb.reshape(depth, 2 * HG_WIDTH)
    out = pl.pallas_call(
        functools.partial(_lbs_kernel, depth=depth),
        out_shape=jax.ShapeDtypeStruct(flat.shape, F32),
        name="hg_lower_bounds",
    )(flat)
    return out.reshape(depth, 2, 1, HG_WIDTH)


def _mods_kernel(c_ref, w_ref, b_ref, o_ref):
    s = _silu(c_ref[...])
    o_ref[0] = _dot(s.astype(BF16), w_ref[0]) + b_ref[0]


def _modulations(cond, w_mod, b_mod):
    depth = w_mod.shape[0]
    nblk = w_mod.shape[2] // D_MODEL
    return pl.pallas_call(
        _mods_kernel,
        grid=(depth, nblk),
        in_specs=[
            pl.BlockSpec((MODS_ROWS, D_MODEL), lambda l, j: (0, 0)),
            pl.BlockSpec((1, D_MODEL, D_MODEL), lambda l, j: (l, 0, j)),
            pl.BlockSpec((1, 1, D_MODEL), lambda l, j: (l, 0, j)),
        ],
        out_specs=pl.BlockSpec((1, MODS_ROWS, D_MODEL), lambda l, j: (l, 0, j)),
        out_shape=jax.ShapeDtypeStruct((depth, MODS_ROWS, nblk * D_MODEL), F32),
        compiler_params=_cparams("arbitrary", "arbitrary"),
        name="adaln_modulations",
    )(cond, w_mod, b_mod.reshape(depth, 1, -1))


def _mod_spec(chunk, nct):
    return pl.BlockSpec((1, 1, D_MODEL),
                        lambda b, i: (2 * b + jnp.where(i >= nct, 1, 0), 0, chunk))


def _row_spec(width):
    return pl.BlockSpec((1, ROW_TILE, width), lambda b, i: (b, i, 0))


def _inproj_hg_kernel(x_ref, sc_ref, sh_ref, w_ref, lb_ref, q_ref, lf_ref, kk_ref, v_ref, g_ref):
    h = (x_ref[0] * (1.0 + sc_ref[0]) + sh_ref[0]).astype(BF16)
    W = HG_WIDTH

    def proj(j):
        return _dot(h, w_ref[:, j * W:(j + 1) * W])

    q_ref[0] = (_silu(proj(0)) * (HG_DK ** -0.5)).astype(BF16)
    for d in range(2):
        lb = lb_ref[d]
        xf = proj(1 + d)
        e = jnp.exp(-jnp.abs(xf))
        r = 1.0 / (1.0 + e)
        sig_pos = jnp.where(xf >= 0, r, e * r)
        sig_neg = jnp.where(xf >= 0, e * r, r)
        lf_ref[0, :, d * W:(d + 1) * W] = jnp.log(lb + (1.0 - lb) * sig_pos)
        kk_ref[0, :, d * W:(d + 1) * W] = ((1.0 - lb) * sig_neg).astype(BF16)
    v_ref[0] = proj(3).astype(BF16)
    g_ref[0] = _silu(proj(4)).astype(BF16)


def _inproj_hg(xs, modsel, w, lb, nct):
    bsz, s, _ = xs.shape
    shapes = [(HG_WIDTH, BF16), (2 * HG_WIDTH, F32), (2 * HG_WIDTH, BF16), (HG_WIDTH, BF16), (HG_WIDTH, BF16)]
    return pl.pallas_call(
        _inproj_hg_kernel,
        grid=(bsz, s // ROW_TILE),
        in_specs=[_row_spec(D_MODEL), _mod_spec(1, nct), _mod_spec(0, nct),
                  _resident(w.shape), _resident(lb.shape)],
        out_specs=[_row_spec(width) for width, _ in shapes],
        out_shape=[jax.ShapeDtypeStruct((bsz, s, width), dt) for width, dt in shapes],
        compiler_params=_cparams("parallel", "arbitrary"),
        name="inproj_hgrn2",
    )(xs, modsel, modsel, w, lb)


def _inproj_rest_kernel(x_ref, xp_ref, xn_ref, sc_ref, sh_ref, wat_ref, wz_ref, wgt_ref, wdt_ref, wx_ref,
                        cos_ref, sin_ref, qn_ref, kn_ref, cw_ref, cb_ref,
                        q_ref, k_ref, v_ref, z_ref, gt_ref, dt_ref, u_ref, pe_s, cv_s, *, nct, nt):
    i = pl.program_id(1)
    scale1 = 1.0 + sc_ref[0]
    shift = sh_ref[0]
    xm = x_ref[0] * scale1 + shift
    h = xm.astype(BF16)

    p = _dot(h, wat_ref[...])
    cos = cos_ref[...]
    sin = sin_ref[...]
    lane = lax.broadcasted_iota(jnp.int32, cos.shape, 1)
    even = (lane % 2) == 0

    def norm_rope(xh, w):
        ms = jnp.mean(xh * xh, axis=-1, keepdims=True)
        xn = xh * lax.rsqrt(ms + RMS_EPS) * w
        partner = jnp.where(even, pltpu.roll(xn, AT_HEAD_DIM - 1, 1), pltpu.roll(xn, 1, 1))
        return xn * cos + partner * sin

    qscale = AT_HEAD_DIM ** -0.5 * LOG2E
    for hh in range(AT_HEADS):
        sl = slice(hh * AT_HEAD_DIM, (hh + 1) * AT_HEAD_DIM)
        q_ref[0, :, sl] = (norm_rope(p[:, sl], qn_ref[...]) * qscale).astype(BF16)
    ones = jnp.ones((ROW_TILE, AT_HEAD_DIM), BF16)
    for kv in range(AT_KV_HEADS):
        sl = slice(kv * AT_HEAD_DIM, (kv + 1) * AT_HEAD_DIM)
        src = slice(AT_WIDTH + kv * AT_HEAD_DIM, AT_WIDTH + (kv + 1) * AT_HEAD_DIM)
        k_ref[0, :, sl] = norm_rope(p[:, src], kn_ref[...]).astype(BF16)
        vsrc = slice(AT_WIDTH + AT_KV_WIDTH + kv * AT_HEAD_DIM, AT_WIDTH + AT_KV_WIDTH + (kv + 1) * AT_HEAD_DIM)
        v_ref[0, :, 2 * kv * AT_HEAD_DIM:(2 * kv + 1) * AT_HEAD_DIM] = p[:, vsrc].astype(BF16)
        v_ref[0, :, (2 * kv + 1) * AT_HEAD_DIM:(2 * kv + 2) * AT_HEAD_DIM] = ones

    z_ref[0] = _silu(_dot(h, wz_ref[...])).astype(BF16)
    gt_ref[0] = _sigmoid(_dot(h, wgt_ref[...])).astype(BF16)
    dt_ref[0] = _dot(h, wdt_ref[...])

    half = MB_CONV // 2
    top = ROW_TILE + SUBLANES
    h_ext = jnp.concatenate([xp_ref[0] * scale1 + shift, xm, xn_ref[0] * scale1 + shift], axis=0).astype(BF16)
    prev_ok = jnp.where((i != 0) & (i != nct), 1.0, 0.0)
    next_ok = jnp.where((i != nct - 1) & (i != nt - 1), 1.0, 0.0)
    for c in range(MB_CONV_DIM // CONV_CH_TILE):
        res = _dot(h_ext, wx_ref[:, c * CONV_CH_TILE:(c + 1) * CONV_CH_TILE])
        for j in range(CONV_CH_TILE // LANES):
            lo = c * CONV_CH_TILE + j * LANES
            pe = pe_s.at[j]
            cv = cv_s.at[j]
            pe[0:SUBLANES, :] = res[0:SUBLANES, j * LANES:(j + 1) * LANES] * prev_ok
            pe[SUBLANES:top, :] = res[SUBLANES:top, j * LANES:(j + 1) * LANES]
            pe[top:top + SUBLANES, :] = res[top:top + SUBLANES, j * LANES:(j + 1) * LANES] * next_ok
            w = [jnp.broadcast_to(cw_ref[t:t + 1, lo:lo + LANES], (SUBLANES, LANES)) for t in range(MB_CONV)]
            bias = jnp.broadcast_to(cb_ref[:, lo:lo + LANES], (SUBLANES, LANES))
            taps = [pe[pl.ds(k, SUBLANES, stride=CONV_STRIDE), :] for k in range(CONV_STRIDE + MB_CONV - 1)]
            for k in range(CONV_STRIDE):
                acc = bias + w[0] * taps[k]
                for t in range(1, MB_CONV):
                    acc = acc + w[t] * taps[k + t]
                cv[pl.ds(k + half, SUBLANES, stride=CONV_STRIDE), :] = _silu(acc)
            u_ref[0, :, lo:lo + LANES] = cv[SUBLANES:top, :].astype(BF16)


def _inproj_rest(xs, modsel, w_at, w_z, w_gt, w_dt, w_xbc, cos, sin, qn, kn, conv_w, conv_b, nct):
    bsz, s, _ = xs.shape
    nt = s // ROW_TILE
    halo_blocks = ROW_TILE // SUBLANES
    last_halo = s // SUBLANES - 1
    shapes = [(AT_WIDTH, BF16), (AT_KV_WIDTH, BF16), (2 * AT_KV_WIDTH, BF16), (MB_INNER, BF16),
              (3 * D_MODEL, BF16), (LANES, F32), (MB_CONV_DIM, BF16)]
    return pl.pallas_call(
        functools.partial(_inproj_rest_kernel, nct=nct, nt=nt),
        grid=(bsz, nt),
        in_specs=[
            _row_spec(D_MODEL),
            pl.BlockSpec((1, SUBLANES, D_MODEL), lambda b, i: (b, jnp.maximum(i * halo_blocks - 1, 0), 0)),
            pl.BlockSpec((1, SUBLANES, D_MODEL), lambda b, i: (b, jnp.minimum((i + 1) * halo_blocks, last_halo), 0)),
            _mod_spec(1, nct), _mod_spec(0, nct),
            _resident(w_at.shape), _resident(w_z.shape), _resident(w_gt.shape), _resident(w_dt.shape),
            _resident(w_xbc.shape),
            pl.BlockSpec((ROW_TILE, AT_HEAD_DIM), lambda b, i: (i, 0)),
            pl.BlockSpec((ROW_TILE, AT_HEAD_DIM), lambda b, i: (i, 0)),
            _resident((1, AT_HEAD_DIM)), _resident((1, AT_HEAD_DIM)),
            _resident(conv_w.shape), _resident((1, MB_CONV_DIM)),
        ],
        out_specs=[_row_spec(width) for width, _ in shapes],
        out_shape=[jax.ShapeDtypeStruct((bsz, s, width), dt) for width, dt in shapes],
        scratch_shapes=[pltpu.VMEM((CONV_CH_TILE // LANES, ROW_TILE + 2 * SUBLANES, LANES), F32),
                        pltpu.VMEM((CONV_CH_TILE // LANES, ROW_TILE + 2 * SUBLANES, LANES), F32)],
        compiler_params=_cparams("parallel", "arbitrary"),
        name="inproj_attn_ssd_gates",
    )(xs, xs, xs, modsel, modsel, w_at, w_z, w_gt, w_dt, w_xbc, cos, sin,
      qn.reshape(1, -1), kn.reshape(1, -1), conv_w, conv_b.reshape(1, -1))


def _attn_kernel(q_ref, k_ref, v_ref, o_ref, *, nct, ctx_len):
    i = pl.program_id(2)

    def attend(keys, vals):
        for hh in range(AT_GROUP):
            sl = slice(hh * AT_HEAD_DIM, (hh + 1) * AT_HEAD_DIM)
            s = _dot_nt(q_ref[0, :, sl], keys)
            m = jnp.max(s, axis=-1, keepdims=True)
            e = jnp.exp2(s - m).astype(BF16)
            ol = _dot(e, vals)
            o_ref[0, :, sl] = (ol[:, :AT_HEAD_DIM] / ol[:, AT_HEAD_DIM:]).astype(o_ref.dtype)

    @pl.when(i < nct)
    def _():
        attend(k_ref[0, :ctx_len, :], v_ref[0, :ctx_len, :])

    @pl.when(i >= nct)
    def _():
        attend(k_ref[0], v_ref[0])


def _attention(q, k, v, nct, ctx_len):
    bsz, s, _ = q.shape
    gw = AT_GROUP * AT_HEAD_DIM
    k_spec = pl.BlockSpec((1, s, AT_HEAD_DIM), lambda b, g, i: (b, 0, g))
    v_spec = pl.BlockSpec((1, s, 2 * AT_HEAD_DIM), lambda b, g, i: (b, 0, g))
    return pl.pallas_call(
        functools.partial(_attn_kernel, nct=nct, ctx_len=ctx_len),
        grid=(bsz, AT_KV_HEADS, s // ROW_TILE),
        in_specs=[pl.BlockSpec((1, ROW_TILE, gw), lambda b, g, i: (b, i, g)), k_spec, v_spec],
        out_specs=pl.BlockSpec((1, ROW_TILE, gw), lambda b, g, i: (b, i, g)),
        out_shape=jax.ShapeDtypeStruct((bsz, s, AT_WIDTH), BF16),
        compiler_params=_cparams("parallel", "parallel", "arbitrary"),
        name="gqa_attention",
    )(q, k, v)


def _visit_to_tile(p, v, nct, nt):
    backward = jnp.where(v < nct, nct - 1 - v, nt - 1 - (v - nct))
    return jnp.where(p == 1, v, backward)


def _hgrn2_kernel(q_ref, lf_ref, kk_ref, v_ref, g_ref, gn_ref, tri_ref, o_ref,
                  b_s, st_s, acc_s, *, nct, nt):
    p = pl.program_id(1)
    v = pl.program_id(2)
    row0 = pl.multiple_of(_visit_to_tile(p, v, nct, nt) * ROW_TILE, ROW_TILE)

    @pl.when(v == 0)
    def _():
        st_s[...] = jnp.zeros_like(st_s)

    for bi in range(HG_BATCH):
        b_s[bi] = _dot2(tri_ref[0], lf_ref[bi])

    def run(fwd):
        for bi in range(HG_BATCH):
            _hgrn2_tile(fwd, row0, q_ref.at[bi], kk_ref.at[bi], v_ref.at[bi],
                        b_s.at[bi], st_s.at[bi], acc_s.at[bi])

    @pl.when(p == 0)
    def _():
        run(False)

    @pl.when(p == 1)
    def _():
        run(True)
        for bi in range(HG_BATCH):
            gate = g_ref[bi].astype(F32)
            for h in range(HG_HEADS):
                sl = slice(h * HG_DV, (h + 1) * HG_DV)
                o = acc_s[bi, pl.ds(row0, ROW_TILE), sl]
                ms = jnp.mean(o * o, axis=-1, keepdims=True)
                o_ref[bi, :, sl] = (o * lax.rsqrt(ms + RMS_EPS) * gn_ref[...] * gate[:, sl]).astype(o_ref.dtype)


def _hgrn2_tile(fwd, row0, q_ref, kk_ref, v_ref, b_s, st_s, acc_s):
    C, SB, W = HG_CHUNK, HG_SUB, HG_WIDTH
    nb = C // SB
    n_chunks = ROW_TILE // C
    end_r = SB - 1 if fwd else 0
    mid_r = SB // 2 - 1 if fwd else SB // 2
    trow = lax.broadcasted_iota(jnp.int32, (C, C), 0)
    tcol = lax.broadcasted_iota(jnp.int32, (C, C), 1)
    blk_gap = (trow // SB - tcol // SB) if fwd else (tcol // SB - trow // SB)
    same_blk = (blk_gap == 0) & ((tcol <= trow) if fwd else (tcol >= trow))
    order = list(range(nb)) if fwd else list(reversed(range(nb)))

    def per_block(rows):
        return jnp.concatenate([jnp.broadcast_to(x, (SB, W)) for x in rows], axis=0)

    zero_row = jnp.zeros((1, W), F32)
    for c in (range(n_chunks) if fwd else reversed(range(n_chunks))):
        r0 = c * C
        u = b_s[r0:r0 + C, :]
        d = [b_s[r0 + R * SB + end_r:r0 + R * SB + end_r + 1, :] for R in range(nb)]
        mid = per_block([b_s[r0 + R * SB + mid_r:r0 + R * SB + mid_r + 1, :] for R in range(nb)])
        d_full = per_block(d)
        before = {}
        run = zero_row
        for R in order:
            before[R] = run
            run = run + d[R]
        total = run
        e_before = per_block([jnp.exp(before[R]) for R in range(nb)])
        e_after = per_block([jnp.exp(total - before[R] - d[R]) for R in range(nb)])

        qq = q_ref[r0:r0 + C, :].astype(F32)
        kk = kk_ref[r0:r0 + C, :].astype(F32)
        q_in = (qq * jnp.exp(u - mid)).astype(BF16)
        k_in = (kk * jnp.exp(mid - u)).astype(BF16)
        q_blk = qq * jnp.exp(u)
        k_blk = kk * jnp.exp(d_full - u)
        k_blk16 = k_blk.astype(BF16)
        q_state = (q_blk * e_before).astype(BF16)
        k_state = (k_blk * e_after).astype(BF16)
        decay = jnp.exp(total)
        q_gap = {1: q_blk.astype(BF16)}
        for gap in range(2, nb):
            between = []
            for R in range(nb):
                J = R - gap if fwd else R + gap
                if 0 <= J < nb:
                    mids = range(J + 1, R) if fwd else range(R + 1, J)
                    g = zero_row
                    for m in mids:
                        g = g + d[m]
                    between.append(g)
                else:
                    between.append(zero_row)
            q_gap[gap] = (q_blk * per_block([jnp.exp(g) for g in between])).astype(BF16)
        vv = v_ref[r0:r0 + C, :]

        outs, states = [], []
        for h in range(HG_HEADS):
            sl = slice(h * HG_DK, (h + 1) * HG_DK)
            a = jnp.where(same_blk, _dot_nt(q_in[:, sl], k_in[:, sl]), 0.0)
            for gap in range(1, nb):
                a = a + jnp.where(blk_gap == gap, _dot_nt(q_gap[gap][:, sl], k_blk16[:, sl]), 0.0)
            st = st_s[h]
            outs.append(_dot(a.astype(BF16), vv[:, sl]) + _dot_nt(q_state[:, sl], st.astype(BF16)))
            states.append(st * decay[:, sl] + _dot_tn(vv[:, sl], k_state[:, sl]))
        for h in range(HG_HEADS):
            st_s[h] = states[h]
        o = jnp.concatenate(outs, axis=1)
        acc_rows = pl.ds(row0 + r0, C)
        if fwd:
            acc_s[acc_rows, :] = acc_s[acc_rows, :] + o
        else:
            acc_s[acc_rows, :] = o


def _hgrn2(qs, lf, kk, vals, gate, gnorm, tri, nct):
    bsz, s, _ = qs.shape
    nt = s // ROW_TILE
    tile = lambda p, v: _visit_to_tile(p, v, nct, nt)
    assert bsz % HG_BATCH == 0
    both = pl.BlockSpec((HG_BATCH, ROW_TILE, HG_WIDTH), lambda b, p, v: (b, tile(p, v), 0))
    by_dir = pl.BlockSpec((HG_BATCH, ROW_TILE, HG_WIDTH), lambda b, p, v: (b, tile(p, v), 1 - p))
    pass1 = pl.BlockSpec((HG_BATCH, ROW_TILE, HG_WIDTH), lambda b, p, v: (b, tile(p, v) * p, 0))
    return pl.pallas_call(
        functools.partial(_hgrn2_kernel, nct=nct, nt=nt),
        grid=(bsz // HG_BATCH, 2, nt),
        in_specs=[
            both, by_dir, by_dir, both, pass1,
            pl.BlockSpec((1, HG_DV), lambda b, p, v: (0, 0)),
            pl.BlockSpec((1, ROW_TILE, ROW_TILE), lambda b, p, v: (p, 0, 0)),
        ],
        out_specs=pass1,
        out_shape=jax.ShapeDtypeStruct((bsz, s, HG_WIDTH), BF16),
        scratch_shapes=[
            pltpu.VMEM((HG_BATCH, ROW_TILE, HG_WIDTH), F32),
            pltpu.VMEM((HG_BATCH, HG_HEADS, HG_DV, HG_DK), F32),
            pltpu.VMEM((HG_BATCH, s, HG_WIDTH), F32),
        ],
        compiler_params=_cparams("parallel", "arbitrary", "arbitrary"),
        name="hgrn2_bidir",
    )(qs, lf, kk, vals, gate, gnorm.reshape(1, -1), tri)


def _ssd_kernel(u_ref, dt_ref, z_ref, dtb_ref, alog_ref, dexp_ref, nw_ref, tril_ref, triu_ref,
                exp_ref, o_ref, st_s, acc_s, *, nct, nt):
    p = pl.program_id(1)
    v = pl.program_id(2)
    row0 = pl.multiple_of(_visit_to_tile(p, v, nct, nt) * MB_CHUNK, MB_CHUNK)
    fwd = p == 1
    L = MB_CHUNK

    @pl.when(v == 0)
    def _():
        st_s[...] = jnp.zeros_like(st_s)

    xs16 = u_ref[0, :, :MB_INNER]
    xs = xs16.astype(F32)
    bm = u_ref[0, :, MB_INNER:MB_INNER + MB_BC_WIDTH]
    cm = u_ref[0, :, MB_INNER + MB_BC_WIDTH:]

    xdt = dt_ref[0] + dtb_ref[...]
    dt = jnp.maximum(xdt, 0.0) + jnp.log(1.0 + jnp.exp(-jnp.abs(xdt)))
    dta = dt * (-jnp.exp(alog_ref[...]))
    cum_f = _dot2(tril_ref[...], dta)
    cum_b = _dot2(triu_ref[...], dta)
    cum = jnp.where(fwd, cum_f, cum_b)
    cum_end = jnp.where(fwd, cum[L - 1:L, :], cum[0:1, :])
    expand = exp_ref[0]
    e_in = _dot(jnp.exp(cum).astype(BF16), expand)
    e_out = _dot((jnp.exp(cum_end - cum) * dt).astype(BF16), expand)
    e_end = jnp.where(fwd, e_in[L - 1:L, :], e_in[0:1, :])
    x_out = (xs * e_out).astype(BF16)

    y_parts = []
    for g in range(MB_GROUPS):
        gsl = slice(g * MB_GROUP_WIDTH, (g + 1) * MB_GROUP_WIDTH)
        nsl = slice(g * MB_STATE, (g + 1) * MB_STATE)
        st = st_s[:, gsl]
        y_parts.append(_dot(cm[:, nsl], st.astype(BF16)) * e_in[:, gsl])
        st_s[:, gsl] = st * e_end[:, gsl] + _dot_tn(bm[:, nsl], x_out[:, gsl])
    y_inter = jnp.concatenate(y_parts, axis=1)

    rows = pl.ds(row0, L)

    @pl.when(p == 0)
    def _():
        acc_s[rows, :] = y_inter

    @pl.when(p == 1)
    def _():
        trow = lax.broadcasted_iota(jnp.int32, (L, L), 0)
        tcol = lax.broadcasted_iota(jnp.int32, (L, L), 1)
        lower = tcol <= trow
        diagonal = tcol == trow
        lane = lax.broadcasted_iota(jnp.int32, (L, LANES), 1)
        first_half = lane < MB_HEAD_DIM
        log_dt = jnp.log(dt)
        col_f = cum_f * LOG2E
        col_b = cum_b * LOG2E
        row_f_t = ((cum_f - log_dt) * LOG2E).T
        row_b_t = ((cum_b - log_dt) * LOG2E).T
        dt_t = dt.T
        pieces = []
        for g in range(MB_GROUPS):
            nsl = slice(g * MB_STATE, (g + 1) * MB_STATE)
            cb = _dot_nt(cm[:, nsl], bm[:, nsl])
            for hp in range(MB_HEADS_PER_GROUP // 2):
                ys = []
                for k in range(2):
                    h = g * MB_HEADS_PER_GROUP + 2 * hp + k
                    hb = MB_HEADS + h
                    seg_f = col_f[:, h:h + 1] - row_f_t[h:h + 1, :]
                    seg_b = col_b[:, hb:hb + 1] - row_b_t[hb:hb + 1, :]
                    e = jnp.exp2(jnp.where(lower, seg_f, seg_b))
                    m = (cb * (e + jnp.where(diagonal, dt_t[hb:hb + 1, :], 0.0))).astype(BF16)
                    c0 = (h // 2) * LANES
                    ys.append(_dot(m, xs16[:, c0:c0 + LANES]))
                pieces.append(jnp.where(first_half, ys[0], ys[1]))
        y = jnp.concatenate(pieces, axis=1) + y_inter + acc_s[rows, :]
        y = (y + dexp_ref[...] * xs) * z_ref[0].astype(F32)
        for g in range(MB_GROUPS):
            gsl = slice(g * MB_GROUP_WIDTH, (g + 1) * MB_GROUP_WIDTH)
            yg = y[:, gsl]
            ms = jnp.mean(yg * yg, axis=-1, keepdims=True)
            o_ref[0, :, gsl] = (yg * lax.rsqrt(ms + RMS_EPS) * nw_ref[:, gsl]).astype(o_ref.dtype)


def _ssd(u, dt_raw, z, dt_bias_row, a_log_row, d_exp, norm_w, tril, triu, expand, nct):
    bsz, s, _ = u.shape
    nt = s // MB_CHUNK
    tile = lambda p, v: _visit_to_tile(p, v, nct, nt)
    return pl.pallas_call(
        functools.partial(_ssd_kernel, nct=nct, nt=nt),
        grid=(bsz, 2, nt),
        in_specs=[
            pl.BlockSpec((1, MB_CHUNK, MB_CONV_DIM), lambda b, p, v: (b, tile(p, v), 0)),
            pl.BlockSpec((1, MB_CHUNK, LANES), lambda b, p, v: (b, tile(p, v), 0)),
            pl.BlockSpec((1, MB_CHUNK, MB_INNER), lambda b, p, v: (b, tile(p, v) * p, 0)),
            pl.BlockSpec((1, LANES), lambda b, p, v: (0, 0)),
            pl.BlockSpec((1, LANES), lambda b, p, v: (0, 0)),
            pl.BlockSpec((1, MB_INNER), lambda b, p, v: (0, 0)),
            pl.BlockSpec((1, MB_INNER), lambda b, p, v: (0, 0)),
            pl.BlockSpec((MB_CHUNK, MB_CHUNK), lambda b, p, v: (0, 0)),
            pl.BlockSpec((MB_CHUNK, MB_CHUNK), lambda b, p, v: (0, 0)),
            pl.BlockSpec((1, LANES, MB_INNER), lambda b, p, v: (p, 0, 0)),
        ],
        out_specs=pl.BlockSpec((1, MB_CHUNK, MB_INNER), lambda b, p, v: (b, tile(p, v) * p, 0)),
        out_shape=jax.ShapeDtypeStruct((bsz, s, MB_INNER), BF16),
        scratch_shapes=[
            pltpu.VMEM((MB_STATE, MB_INNER), F32),
            pltpu.VMEM((s, MB_INNER), F32),
        ],
        compiler_params=_cparams("parallel", "arbitrary", "arbitrary"),
        name="ssd_bidir",
    )(u, dt_raw, z, dt_bias_row, a_log_row, d_exp, norm_w, tril, triu, expand)


def _merge_kernel(x_ref, ohg_ref, oat_ref, omb_ref, gt_ref, g1_ref, whg_ref, wat_ref, wmb_ref,
                  wout_ref, lng_ref, lnb_ref, o_ref, *, alpha):
    gates = gt_ref[0].astype(F32)
    y = (gates[:, :D_MODEL] * _dot(ohg_ref[0], whg_ref[...])
         + gates[:, D_MODEL:2 * D_MODEL] * _dot(oat_ref[0], wat_ref[...])
         + gates[:, 2 * D_MODEL:] * _dot(omb_ref[0], wmb_ref[...]))
    y = _dot(y.astype(BF16), wout_ref[...])
    o_ref[0] = _layer_norm(alpha * x_ref[0] + g1_ref[0] * y, lng_ref[...], lnb_ref[...])


def _merge(xs, o_hg, o_at, o_mb, gates, modsel, w_hg, w_at, w_mb, w_out, ln_g, ln_b, nct, alpha, skip):
    bsz, s, _ = xs.shape
    row = lambda width: pl.BlockSpec((1, ROW_TILE, width), lambda b, i: (b, i + skip, 0))
    return pl.pallas_call(
        functools.partial(_merge_kernel, alpha=alpha),
        grid=(bsz, s // ROW_TILE - skip),
        in_specs=[row(D_MODEL), row(HG_WIDTH), row(AT_WIDTH), row(MB_INNER), row(3 * D_MODEL),
                  _mod_spec(2, nct - skip),
                  _resident(w_hg.shape), _resident(w_at.shape), _resident(w_mb.shape),
                  _resident(w_out.shape), _resident((1, D_MODEL)), _resident((1, D_MODEL))],
        out_specs=_row_spec(D_MODEL),
        out_shape=jax.ShapeDtypeStruct((bsz, s - skip * ROW_TILE, D_MODEL), F32),
        compiler_params=_cparams("parallel", "arbitrary"),
        name="merge_out_ln",
    )(xs, o_hg, o_at, o_mb, gates, modsel, w_hg, w_at, w_mb, w_out,
      ln_g.reshape(1, -1), ln_b.reshape(1, -1))


def _ffn_kernel(x_ref, sc_ref, sh_ref, g2_ref, win_ref, wout_ref, lng_ref, lnb_ref, o_ref, *, alpha):
    x = x_ref[0]
    h = x * (1.0 + sc_ref[0]) + sh_ref[0]
    gu = _dot(h.astype(BF16), win_ref[...])
    a = _silu(gu[:, :FFN_HIDDEN]) * gu[:, FFN_HIDDEN:]
    y = _dot(a.astype(BF16), wout_ref[...])
    o_ref[0] = _layer_norm(alpha * x + g2_ref[0] * y, lng_ref[...], lnb_ref[...])


def _ffn(xs, modsel, w_in, w_out, ln_g, ln_b, nct, alpha):
    bsz, s, _ = xs.shape
    row = pl.BlockSpec((1, ROW_TILE, D_MODEL), lambda b, i: (b, i, 0))
    return pl.pallas_call(
        functools.partial(_ffn_kernel, alpha=alpha),
        grid=(bsz, s // ROW_TILE),
        in_specs=[row, _mod_spec(4, nct), _mod_spec(3, nct), _mod_spec(5, nct),
                  _resident(w_in.shape), _resident(w_out.shape),
                  _resident((1, D_MODEL)), _resident((1, D_MODEL))],
        out_specs=row,
        out_shape=jax.ShapeDtypeStruct(xs.shape, F32),
        compiler_params=_cparams("parallel", "arbitrary"),
        name="swiglu_ffn_ln",
    )(xs, modsel, modsel, modsel, w_in, w_out, ln_g.reshape(1, -1), ln_b.reshape(1, -1))


def _rope_tables(seq, ctx_len):
    rows = seq // GRID_W
    row, col = jnp.meshgrid(jnp.arange(rows, dtype=F32), jnp.arange(GRID_W, dtype=F32), indexing="ij")
    n_pairs = AT_HEAD_DIM // 4
    inv_freq = ROPE_THETA ** (-jnp.arange(n_pairs, dtype=F32) / n_pairs)
    ang = jnp.concatenate([row.reshape(-1, 1) * inv_freq, col.reshape(-1, 1) * inv_freq], axis=-1)
    cos = jnp.repeat(jnp.cos(ang), 2, axis=-1)
    sin = jnp.repeat(jnp.sin(ang), 2, axis=-1) * jnp.tile(jnp.array([-1.0, 1.0], F32), AT_HEAD_DIM // 2)
    cos = jnp.concatenate([jnp.ones((ctx_len, AT_HEAD_DIM), F32), cos], axis=0)
    sin = jnp.concatenate([jnp.zeros((ctx_len, AT_HEAD_DIM), F32), sin], axis=0)
    return cos, sin


def _block_tri(n, blk):
    r = jnp.arange(n)
    same = (r[:, None] // blk) == (r[None, :] // blk)
    lower = (same & (r[None, :] <= r[:, None])).astype(BF16)
    upper = (same & (r[None, :] >= r[:, None])).astype(BF16)
    return lower, upper


def _head_expand():
    lane = jnp.arange(LANES)[:, None]
    head = jnp.arange(MB_INNER)[None, :] // MB_HEAD_DIM
    fwd = (lane == head).astype(BF16)
    bwd = (lane == head + MB_HEADS).astype(BF16)
    return jnp.stack([bwd, fwd])


def _pad_lanes(row):
    return jnp.pad(row.reshape(1, -1), ((0, 0), (0, LANES - row.size)))


def kernel(x, c, ctx, c_ctx, w_mod, b_mod, w_in, hg_lb, hg_gnorm, at_qnorm, at_knorm,
           mb_conv_w, mb_conv_b, mb_dt_bias, mb_a_log, mb_d, mb_norm,
           w_br_hg, w_br_at, w_br_mb, w_out, ln1_g, ln1_b, w_ffn_in, w_ffn_out, ln2_g, ln2_b):
    bsz, seq, _ = x.shape
    ctx_len = ctx.shape[1]
    depth = w_mod.shape[0]
    assert seq % ROW_TILE == 0 and ctx_len % ROW_TILE == 0 and seq % GRID_W == 0
    assert bsz + 1 <= MODS_ROWS
    alpha = (2 * depth) ** 0.25
    nct = ctx_len // ROW_TILE
    nct_mb = ctx_len // MB_CHUNK

    cos, sin = _rope_tables(seq, ctx_len)
    hg_tril, hg_triu = _block_tri(ROW_TILE, HG_SUB)
    hg_tri = jnp.stack([hg_triu, hg_tril])
    mb_tril, mb_triu = _block_tri(MB_CHUNK, MB_CHUNK)
    expand = _head_expand()

    lbs = _lower_bounds(hg_lb)
    cond = jnp.concatenate([c, c_ctx[None, :], jnp.zeros((MODS_ROWS - bsz - 1, D_MODEL), F32)], axis=0)
    mods = _modulations(cond, w_mod.astype(BF16), b_mod)
    ctx_rows = jnp.broadcast_to(mods[:, bsz:bsz + 1, :], (depth, bsz, mods.shape[-1]))
    modsel_all = jnp.stack([ctx_rows, mods[:, :bsz, :]], axis=2).reshape(depth, 2 * bsz, 1, -1)

    splits = [0]
    for width in IN_SIZES:
        splits.append(splits[-1] + width)
    o_hg, o_at, o_z, o_xbc, o_dt, o_gt = splits[0], splits[5], splits[8], splits[9], splits[10], splits[11]

    xs = jnp.concatenate([ctx, x], axis=1)
    for l in range(depth):
        modsel = modsel_all[l]
        w_cols = lambda a, b: w_in[l, :, a:b].astype(BF16)
        w_dt_in = jnp.pad(w_cols(o_dt, o_gt), ((0, 0), (0, LANES - (o_gt - o_dt))))

        qs, lf, kk, vals, hg_gate = _inproj_hg(xs, modsel, w_cols(o_hg, o_at), lbs[l], nct)
        q, k, v, z_act, gt_act, dt_raw, u = _inproj_rest(
            xs, modsel, w_cols(o_at, o_z), w_cols(o_z, o_xbc), w_cols(o_gt, splits[-1]), w_dt_in,
            w_cols(o_xbc, o_dt), cos, sin, at_qnorm[l], at_knorm[l], mb_conv_w[l], mb_conv_b[l], nct)

        out_hg = _hgrn2(qs, lf, kk, vals, hg_gate, hg_gnorm[l], hg_tri, nct)
        out_at = _attention(q, k, v, nct, ctx_len)
        out_mb = _ssd(u, dt_raw, z_act, _pad_lanes(mb_dt_bias[l]), _pad_lanes(mb_a_log[l]),
                      jnp.repeat(mb_d[l], MB_HEAD_DIM).reshape(1, -1), mb_norm[l].reshape(1, -1),
                      mb_tril, mb_triu, expand, nct_mb)

        skip = nct if l == depth - 1 else 0
        xs = _merge(xs, out_hg, out_at, out_mb, gt_act, modsel,
                    w_br_hg[l].astype(BF16), w_br_at[l].astype(BF16), w_br_mb[l].astype(BF16),
                    w_out[l].astype(BF16), ln1_g[l], ln1_b[l], nct, alpha, skip)
        xs = _ffn(xs, modsel, w_ffn_in[l].astype(BF16), w_ffn_out[l].astype(BF16),
                  ln2_g[l], ln2_b[l], nct - skip, alpha)
    return xs
```

```python
import functools
import math

import jax
import jax.numpy as jnp
from jax import lax
from jax.experimental import pallas as pl
from jax.experimental.pallas import tpu as pltpu

F32 = jnp.float32
BF16 = jnp.bfloat16

D_MODEL = 1024
HG_HEADS = 8
HG_DK = 128
HG_DV = 128
HG_WIDTH = HG_HEADS * HG_DK
AT_HEADS = 8
AT_KV_HEADS = 2
AT_GROUP = AT_HEADS // AT_KV_HEADS
AT_HEAD_DIM = 128
AT_WIDTH = AT_HEADS * AT_HEAD_DIM
AT_KV_WIDTH = AT_KV_HEADS * AT_HEAD_DIM
GRID_W = 64
ROPE_THETA = 10000.0
MB_INNER = 2 * D_MODEL
MB_HEAD_DIM = 64
MB_HEADS = MB_INNER // MB_HEAD_DIM
MB_STATE = 128
MB_GROUPS = 4
MB_HEADS_PER_GROUP = MB_HEADS // MB_GROUPS
MB_CONV = 5
MB_BC_WIDTH = MB_GROUPS * MB_STATE
MB_CONV_DIM = MB_INNER + 2 * MB_BC_WIDTH
MB_GROUP_WIDTH = MB_INNER // MB_GROUPS
FFN_HIDDEN = ((8 * D_MODEL + 3 * 256 - 1) // (3 * 256)) * 256
IN_SIZES = (HG_WIDTH,) * 5 + (AT_WIDTH, AT_KV_WIDTH, AT_KV_WIDTH) + (
    MB_INNER, MB_CONV_DIM, 2 * MB_HEADS) + (3 * D_MODEL,)
LN_EPS = 1e-5
RMS_EPS = 1e-6
LOG2E = 1.4426950408889634

LANES = 128
SUBLANES = 8
VMEM_LIMIT_BYTES = 56 * 1024 * 1024

ROW_TILE = 256
HG_CHUNK = 64
HG_SUB = 32
HG_BATCH = 2
MB_BATCH = 2
MB_CHUNK = 128
MODS_ROWS = 24
CONV_CH_TILE = 512
CONV_STRIDE = (ROW_TILE + SUBLANES) // SUBLANES


def _cparams(*sem):
    return pltpu.CompilerParams(dimension_semantics=sem, vmem_limit_bytes=VMEM_LIMIT_BYTES)


def _resident(shape):
    nd = len(shape)
    return pl.BlockSpec(shape, lambda *_: (0,) * nd, pipeline_mode=pl.Buffered(1))


def _sigmoid(x):
    return 1.0 / (1.0 + jnp.exp(-x))


def _silu(x):
    return x * _sigmoid(x)


def _split_bf16(x):
    hi = x.astype(BF16)
    lo = (x - hi.astype(F32)).astype(BF16)
    return hi, lo


def _dot(a, b):
    return jnp.dot(a, b, preferred_element_type=F32)


def _dot_nt(a, b):
    return lax.dot_general(a, b, (((1,), (1,)), ((), ())), preferred_element_type=F32)


def _dot_tn(a, b):
    return lax.dot_general(a, b, (((0,), (0,)), ((), ())), preferred_element_type=F32)


def _dot2(m_bf16, x):
    hi, lo = _split_bf16(x)
    return _dot(m_bf16, hi) + _dot(m_bf16, lo)


def _dot2_right(x, m_bf16):
    hi, lo = _split_bf16(x)
    return _dot(hi, m_bf16) + _dot(lo, m_bf16)


def _layer_norm(v, g, b):
    mu = jnp.mean(v, axis=-1, keepdims=True)
    c = v - mu
    var = jnp.mean(c * c, axis=-1, keepdims=True)
    return c * lax.rsqrt(var + LN_EPS) * g + b


def _lbs_kernel(lb_ref, o_ref, *, depth):
    x = lb_ref[...]
    m = jnp.max(x, axis=0, keepdims=True)
    e = jnp.exp(x - m)
    p = e / jnp.sum(e, axis=0, keepdims=True)
    acc = p[0:1]
    for l in range(depth):
        if l > 0:
            acc = acc + p[l:l + 1]
        o_ref[l:l + 1, :] = acc - p[0:1]


def _lower_bounds(hg_lb):
    depth = hg_lb.shape[0]
    flat = hg_lb.reshape(depth, 2 * HG_WIDTH)
    out = pl.pallas_call(
        functools.partial(_lbs_kernel, depth=depth),
        out_shape=jax.ShapeDtypeStruct(flat.shape, F32),
        name="hg_lower_bounds",
    )(flat)
    return out.reshape(depth, 2, 1, HG_WIDTH)


def _mods_kernel(c_ref, w_ref, b_ref, o_ref):
    s = _silu(c_ref[...])
    o_ref[0] = _dot(s.astype(BF16), w_ref[0]) + b_ref[0]


def _modulations(cond, w_mod, b_mod):
    depth = w_mod.shape[0]
    nblk = w_mod.shape[2] // D_MODEL
    return pl.pallas_call(
        _mods_kernel,
        grid=(depth, nblk),
        in_specs=[
            pl.BlockSpec((MODS_ROWS, D_MODEL), lambda l, j: (0, 0)),
            pl.BlockSpec((1, D_MODEL, D_MODEL), lambda l, j: (l, 0, j)),
            pl.BlockSpec((1, 1, D_MODEL), lambda l, j: (l, 0, j)),
        ],
        out_specs=pl.BlockSpec((1, MODS_ROWS, D_MODEL), lambda l, j: (l, 0, j)),
        out_shape=jax.ShapeDtypeStruct((depth, MODS_ROWS, nblk * D_MODEL), F32),
        compiler_params=_cparams("arbitrary", "arbitrary"),
        name="adaln_modulations",
    )(cond, w_mod, b_mod.reshape(depth, 1, -1))


def _mod_spec(chunk, nct):
    return pl.BlockSpec((1, 1, D_MODEL),
                        lambda b, i: (2 * b + jnp.where(i >= nct, 1, 0), 0, chunk))


def _row_spec(width):
    return pl.BlockSpec((1, ROW_TILE, width), lambda b, i: (b, i, 0))


def _inproj_hg_kernel(x_ref, sc_ref, sh_ref, w_ref, lb_ref, q_ref, lf_ref, kk_ref, v_ref, g_ref):
    h = (x_ref[0] * (1.0 + sc_ref[0]) + sh_ref[0]).astype(BF16)
    W = HG_WIDTH

    def proj(j):
        return _dot(h, w_ref[:, j * W:(j + 1) * W])

    q_ref[0] = (_silu(proj(0)) * (HG_DK ** -0.5)).astype(BF16)
    for d in range(2):
        lb = lb_ref[d]
        xf = proj(1 + d)
        e = jnp.exp(-jnp.abs(xf))
        r = 1.0 / (1.0 + e)
        sig_pos = jnp.where(xf >= 0, r, e * r)
        sig_neg = jnp.where(xf >= 0, e * r, r)
        lf_ref[0, :, d * W:(d + 1) * W] = jnp.log(lb + (1.0 - lb) * sig_pos)
        kk_ref[0, :, d * W:(d + 1) * W] = ((1.0 - lb) * sig_neg).astype(BF16)
    v_ref[0] = proj(3).astype(BF16)
    g_ref[0] = _silu(proj(4)).astype(BF16)


def _inproj_hg(xs, modsel, w, lb, nct):
    bsz, s, _ = xs.shape
    shapes = [(HG_WIDTH, BF16), (2 * HG_WIDTH, F32), (2 * HG_WIDTH, BF16), (HG_WIDTH, BF16), (HG_WIDTH, BF16)]
    return pl.pallas_call(
        _inproj_hg_kernel,
        grid=(bsz, s // ROW_TILE),
        in_specs=[_row_spec(D_MODEL), _mod_spec(1, nct), _mod_spec(0, nct),
                  _resident(w.shape), _resident(lb.shape)],
        out_specs=[_row_spec(width) for width, _ in shapes],
        out_shape=[jax.ShapeDtypeStruct((bsz, s, width), dt) for width, dt in shapes],
        compiler_params=_cparams("parallel", "arbitrary"),
        name="inproj_hgrn2",
    )(xs, modsel, modsel, w, lb)


def _inproj_rest_kernel(x_ref, xp_ref, xn_ref, sc_ref, sh_ref, wat_ref, wz_ref, wgt_ref, wdt_ref, wx_ref,
                        cos_ref, sin_ref, qn_ref, kn_ref, cw_ref, cb_ref,
                        q_ref, k_ref, v_ref, z_ref, gt_ref, dt_ref, u_ref, pe_s, cv_s, *, nct, nt):
    i = pl.program_id(1)
    scale1 = 1.0 + sc_ref[0]
    shift = sh_ref[0]
    xm = x_ref[0] * scale1 + shift
    h = xm.astype(BF16)

    p = _dot(h, wat_ref[...])
    cos = cos_ref[...]
    sin = sin_ref[...]
    lane = lax.broadcasted_iota(jnp.int32, cos.shape, 1)
    even = (lane % 2) == 0

    def norm_rope(xh, w):
        ms = jnp.mean(xh * xh, axis=-1, keepdims=True)
        xn = xh * lax.rsqrt(ms + RMS_EPS) * w
        partner = jnp.where(even, pltpu.roll(xn, AT_HEAD_DIM - 1, 1), pltpu.roll(xn, 1, 1))
        return xn * cos + partner * sin

    qscale = AT_HEAD_DIM ** -0.5 * LOG2E
    for hh in range(AT_HEADS):
        sl = slice(hh * AT_HEAD_DIM, (hh + 1) * AT_HEAD_DIM)
        q_ref[0, :, sl] = (norm_rope(p[:, sl], qn_ref[...]) * qscale).astype(BF16)
    ones = jnp.ones((ROW_TILE, AT_HEAD_DIM), BF16)
    for kv in range(AT_KV_HEADS):
        sl = slice(kv * AT_HEAD_DIM, (kv + 1) * AT_HEAD_DIM)
        src = slice(AT_WIDTH + kv * AT_HEAD_DIM, AT_WIDTH + (kv + 1) * AT_HEAD_DIM)
        k_ref[0, :, sl] = norm_rope(p[:, src], kn_ref[...]).astype(BF16)
        vsrc = slice(AT_WIDTH + AT_KV_WIDTH + kv * AT_HEAD_DIM, AT_WIDTH + AT_KV_WIDTH + (kv + 1) * AT_HEAD_DIM)
        v_ref[0, :, 2 * kv * AT_HEAD_DIM:(2 * kv + 1) * AT_HEAD_DIM] = p[:, vsrc].astype(BF16)
        v_ref[0, :, (2 * kv + 1) * AT_HEAD_DIM:(2 * kv + 2) * AT_HEAD_DIM] = ones

    z_ref[0] = _silu(_dot(h, wz_ref[...])).astype(BF16)
    gt_ref[0] = _sigmoid(_dot(h, wgt_ref[...])).astype(BF16)
    dt_ref[0] = _dot(h, wdt_ref[...])

    half = MB_CONV // 2
    top = ROW_TILE + SUBLANES
    h_ext = jnp.concatenate([xp_ref[0] * scale1 + shift, xm, xn_ref[0] * scale1 + shift], axis=0).astype(BF16)
    prev_ok = jnp.where((i != 0) & (i != nct), 1.0, 0.0)
    next_ok = jnp.where((i != nct - 1) & (i != nt - 1), 1.0, 0.0)
    for c in range(MB_CONV_DIM // CONV_CH_TILE):
        res = _dot(h_ext, wx_ref[:, c * CONV_CH_TILE:(c + 1) * CONV_CH_TILE])
        for j in range(CONV_CH_TILE // LANES):
            lo = c * CONV_CH_TILE + j * LANES
            pe = pe_s.at[j]
            cv = cv_s.at[j]
            pe[0:SUBLANES, :] = res[0:SUBLANES, j * LANES:(j + 1) * LANES] * prev_ok
            pe[SUBLANES:top, :] = res[SUBLANES:top, j * LANES:(j + 1) * LANES]
            pe[top:top + SUBLANES, :] = res[top:top + SUBLANES, j * LANES:(j + 1) * LANES] * next_ok
            w = [jnp.broadcast_to(cw_ref[t:t + 1, lo:lo + LANES], (SUBLANES, LANES)) for t in range(MB_CONV)]
            bias = jnp.broadcast_to(cb_ref[:, lo:lo + LANES], (SUBLANES, LANES))
            taps = [pe[pl.ds(k, SUBLANES, stride=CONV_STRIDE), :] for k in range(CONV_STRIDE + MB_CONV - 1)]
            for k in range(CONV_STRIDE):
                acc = bias + w[0] * taps[k]
                for t in range(1, MB_CONV):
                    acc = acc + w[t] * taps[k + t]
                cv[pl.ds(k + half, SUBLANES, stride=CONV_STRIDE), :] = _silu(acc)
            u_ref[0, :, lo:lo + LANES] = cv[SUBLANES:top, :].astype(BF16)


def _inproj_rest(xs, modsel, w_at, w_z, w_gt, w_dt, w_xbc, cos, sin, qn, kn, conv_w, conv_b, nct):
    bsz, s, _ = xs.shape
    nt = s // ROW_TILE
    halo_blocks = ROW_TILE // SUBLANES
    last_halo = s // SUBLANES - 1
    shapes = [(AT_WIDTH, BF16), (AT_KV_WIDTH, BF16), (2 * AT_KV_WIDTH, BF16), (MB_INNER, BF16),
              (3 * D_MODEL, BF16), (LANES, F32), (MB_CONV_DIM, BF16)]
    return pl.pallas_call(
        functools.partial(_inproj_rest_kernel, nct=nct, nt=nt),
        grid=(bsz, nt),
        in_specs=[
            _row_spec(D_MODEL),
            pl.BlockSpec((1, SUBLANES, D_MODEL), lambda b, i: (b, jnp.maximum(i * halo_blocks - 1, 0), 0)),
            pl.BlockSpec((1, SUBLANES, D_MODEL), lambda b, i: (b, jnp.minimum((i + 1) * halo_blocks, last_halo), 0)),
            _mod_spec(1, nct), _mod_spec(0, nct),
            _resident(w_at.shape), _resident(w_z.shape), _resident(w_gt.shape), _resident(w_dt.shape),
            _resident(w_xbc.shape),
            pl.BlockSpec((ROW_TILE, AT_HEAD_DIM), lambda b, i: (i, 0)),
            pl.BlockSpec((ROW_TILE, AT_HEAD_DIM), lambda b, i: (i, 0)),
            _resident((1, AT_HEAD_DIM)), _resident((1, AT_HEAD_DIM)),
            _resident(conv_w.shape), _resident((1, MB_CONV_DIM)),
        ],
        out_specs=[_row_spec(width) for width, _ in shapes],
        out_shape=[jax.ShapeDtypeStruct((bsz, s, width), dt) for width, dt in shapes],
        scratch_shapes=[pltpu.VMEM((CONV_CH_TILE // LANES, ROW_TILE + 2 * SUBLANES, LANES), F32),
                        pltpu.VMEM((CONV_CH_TILE // LANES, ROW_TILE + 2 * SUBLANES, LANES), F32)],
        compiler_params=_cparams("parallel", "arbitrary"),
        name="inproj_attn_ssd_gates",
    )(xs, xs, xs, modsel, modsel, w_at, w_z, w_gt, w_dt, w_xbc, cos, sin,
      qn.reshape(1, -1), kn.reshape(1, -1), conv_w, conv_b.reshape(1, -1))


def _attn_kernel(q_ref, k_ref, v_ref, o_ref, *, nct, ctx_len):
    i = pl.program_id(2)

    def attend(keys, vals):
        for hh in range(AT_GROUP):
            sl = slice(hh * AT_HEAD_DIM, (hh + 1) * AT_HEAD_DIM)
            s = _dot_nt(q_ref[0, :, sl], keys)
            m = jnp.max(s, axis=-1, keepdims=True)
            e = jnp.exp2(s - m).astype(BF16)
            ol = _dot(e, vals)
            o_ref[0, :, sl] = (ol[:, :AT_HEAD_DIM] / ol[:, AT_HEAD_DIM:]).astype(o_ref.dtype)

    @pl.when(i < nct)
    def _():
        attend(k_ref[0, :ctx_len, :], v_ref[0, :ctx_len, :])

    @pl.when(i >= nct)
    def _():
        attend(k_ref[0], v_ref[0])


def _attention(q, k, v, nct, ctx_len):
    bsz, s, _ = q.shape
    gw = AT_GROUP * AT_HEAD_DIM
    k_spec = pl.BlockSpec((1, s, AT_HEAD_DIM), lambda b, g, i: (b, 0, g))
    v_spec = pl.BlockSpec((1, s, 2 * AT_HEAD_DIM), lambda b, g, i: (b, 0, g))
    return pl.pallas_call(
        functools.partial(_attn_kernel, nct=nct, ctx_len=ctx_len),
        grid=(bsz, AT_KV_HEADS, s // ROW_TILE),
        in_specs=[pl.BlockSpec((1, ROW_TILE, gw), lambda b, g, i: (b, i, g)), k_spec, v_spec],
        out_specs=pl.BlockSpec((1, ROW_TILE, gw), lambda b, g, i: (b, i, g)),
        out_shape=jax.ShapeDtypeStruct((bsz, s, AT_WIDTH), BF16),
        compiler_params=_cparams("parallel", "parallel", "arbitrary"),
        name="gqa_attention",
    )(q, k, v)


def _visit_to_tile(p, v, nct, nt):
    backward = jnp.where(v < nct, nct - 1 - v, nt - 1 - (v - nct))
    return jnp.where(p == 1, v, backward)


def _hgrn2_kernel(q_ref, lf_ref, kk_ref, v_ref, g_ref, gn_ref, tri_ref, o_ref,
                  b_s, st_s, acc_s, *, nct, nt):
    p = pl.program_id(1)
    v = pl.program_id(2)
    row0 = pl.multiple_of(_visit_to_tile(p, v, nct, nt) * ROW_TILE, ROW_TILE)

    @pl.when(v == 0)
    def _():
        st_s[...] = jnp.zeros_like(st_s)

    for bi in range(HG_BATCH):
        b_s[bi] = _dot2(tri_ref[0], lf_ref[bi])

    def run(fwd):
        for bi in range(HG_BATCH):
            _hgrn2_tile(fwd, row0, q_ref.at[bi], kk_ref.at[bi], v_ref.at[bi],
                        b_s.at[bi], st_s.at[bi], acc_s.at[bi])

    @pl.when(p == 0)
    def _():
        run(False)

    @pl.when(p == 1)
    def _():
        run(True)
        for bi in range(HG_BATCH):
            gate = g_ref[bi].astype(F32)
            for h in range(HG_HEADS):
                sl = slice(h * HG_DV, (h + 1) * HG_DV)
                o = acc_s[bi, pl.ds(row0, ROW_TILE), sl]
                ms = jnp.mean(o * o, axis=-1, keepdims=True)
                o_ref[bi, :, sl] = (o * lax.rsqrt(ms + RMS_EPS) * gn_ref[...] * gate[:, sl]).astype(o_ref.dtype)


def _hgrn2_tile(fwd, row0, q_ref, kk_ref, v_ref, b_s, st_s, acc_s):
    C, SB, W = HG_CHUNK, HG_SUB, HG_WIDTH
    nb = C // SB
    n_chunks = ROW_TILE // C
    end_r = SB - 1 if fwd else 0
    mid_r = SB // 2 - 1 if fwd else SB // 2
    trow = lax.broadcasted_iota(jnp.int32, (C, C), 0)
    tcol = lax.broadcasted_iota(jnp.int32, (C, C), 1)
    blk_gap = (trow // SB - tcol // SB) if fwd else (tcol // SB - trow // SB)
    same_blk = (blk_gap == 0) & ((tcol <= trow) if fwd else (tcol >= trow))
    order = list(range(nb)) if fwd else list(reversed(range(nb)))

    def per_block(rows):
        return jnp.concatenate([jnp.broadcast_to(x, (SB, W)) for x in rows], axis=0)

    zero_row = jnp.zeros((1, W), F32)
    for c in (range(n_chunks) if fwd else reversed(range(n_chunks))):
        r0 = c * C
        u = b_s[r0:r0 + C, :]
        d = [b_s[r0 + R * SB + end_r:r0 + R * SB + end_r + 1, :] for R in range(nb)]
        mid = per_block([b_s[r0 + R * SB + mid_r:r0 + R * SB + mid_r + 1, :] for R in range(nb)])
        d_full = per_block(d)
        before = {}
        run = zero_row
        for R in order:
            before[R] = run
            run = run + d[R]
        total = run
        e_before = per_block([jnp.exp(before[R]) for R in range(nb)])
        e_after = per_block([jnp.exp(total - before[R] - d[R]) for R in range(nb)])

        qq = q_ref[r0:r0 + C, :].astype(F32)
        kk = kk_ref[r0:r0 + C, :].astype(F32)
        q_in = (qq * jnp.exp(u - mid)).astype(BF16)
        k_in = (kk * jnp.exp(mid - u)).astype(BF16)
        q_blk = qq * jnp.exp(u)
        k_blk = kk * jnp.exp(d_full - u)
        k_blk16 = k_blk.astype(BF16)
        q_state = (q_blk * e_before).astype(BF16)
        k_state = (k_blk * e_after).astype(BF16)
        decay = jnp.exp(total)
        q_gap = {1: q_blk.astype(BF16)}
        for gap in range(2, nb):
            between = []
            for R in range(nb):
                J = R - gap if fwd else R + gap
                if 0 <= J < nb:
                    mids = range(J + 1, R) if fwd else range(R + 1, J)
                    g = zero_row
                    for m in mids:
                        g = g + d[m]
                    between.append(g)
                else:
                    between.append(zero_row)
            q_gap[gap] = (q_blk * per_block([jnp.exp(g) for g in between])).astype(BF16)
        vv = v_ref[r0:r0 + C, :]

        outs, states = [], []
        for h in range(HG_HEADS):
            sl = slice(h * HG_DK, (h + 1) * HG_DK)
            a = jnp.where(same_blk, _dot_nt(q_in[:, sl], k_in[:, sl]), 0.0)
            for gap in range(1, nb):
                a = a + jnp.where(blk_gap == gap, _dot_nt(q_gap[gap][:, sl], k_blk16[:, sl]), 0.0)
            st = st_s[h]
            outs.append(_dot(a.astype(BF16), vv[:, sl]) + _dot_nt(q_state[:, sl], st.astype(BF16)))
            states.append(st * decay[:, sl] + _dot_tn(vv[:, sl], k_state[:, sl]))
        for h in range(HG_HEADS):
            st_s[h] = states[h]
        o = jnp.concatenate(outs, axis=1)
        acc_rows = pl.ds(row0 + r0, C)
        if fwd:
            acc_s[acc_rows, :] = acc_s[acc_rows, :] + o
        else:
            acc_s[acc_rows, :] = o


def _hgrn2(qs, lf, kk, vals, gate, gnorm, tri, nct):
    bsz, s, _ = qs.shape
    nt = s // ROW_TILE
    tile = lambda p, v: _visit_to_tile(p, v, nct, nt)
    assert bsz % HG_BATCH == 0
    both = pl.BlockSpec((HG_BATCH, ROW_TILE, HG_WIDTH), lambda b, p, v: (b, tile(p, v), 0))
    by_dir = pl.BlockSpec((HG_BATCH, ROW_TILE, HG_WIDTH), lambda b, p, v: (b, tile(p, v), 1 - p))
    pass1 = pl.BlockSpec((HG_BATCH, ROW_TILE, HG_WIDTH), lambda b, p, v: (b, tile(p, v) * p, 0))
    return pl.pallas_call(
        functools.partial(_hgrn2_kernel, nct=nct, nt=nt),
        grid=(bsz // HG_BATCH, 2, nt),
        in_specs=[
            both, by_dir, by_dir, both, pass1,
            pl.BlockSpec((1, HG_DV), lambda b, p, v: (0, 0)),
            pl.BlockSpec((1, ROW_TILE, ROW_TILE), lambda b, p, v: (p, 0, 0)),
        ],
        out_specs=pass1,
        out_shape=jax.ShapeDtypeStruct((bsz, s, HG_WIDTH), BF16),
        scratch_shapes=[
            pltpu.VMEM((HG_BATCH, ROW_TILE, HG_WIDTH), F32),
            pltpu.VMEM((HG_BATCH, HG_HEADS, HG_DV, HG_DK), F32),
            pltpu.VMEM((HG_BATCH, s, HG_WIDTH), F32),
        ],
        compiler_params=_cparams("parallel", "arbitrary", "arbitrary"),
        name="hgrn2_bidir",
    )(qs, lf, kk, vals, gate, gnorm.reshape(1, -1), tri)


def _ssd_kernel(u_ref, dt_ref, z_ref, dtb_ref, alog_ref, dexp_ref, nw_ref, tril_ref, triu_ref,
                exp_ref, o_ref, st_s, acc_s, *, nct, nt):
    p = pl.program_id(1)
    v = pl.program_id(2)
    row0 = pl.multiple_of(_visit_to_tile(p, v, nct, nt) * MB_CHUNK, MB_CHUNK)
    fwd = p == 1
    L = MB_CHUNK

    @pl.when(v == 0)
    def _():
        st_s[...] = jnp.zeros_like(st_s)

    expand = exp_ref[0]
    per_seq = []
    for bi in range(MB_BATCH):
        xs16 = u_ref[bi, :, :MB_INNER]
        xs = xs16.astype(F32)
        bm = u_ref[bi, :, MB_INNER:MB_INNER + MB_BC_WIDTH]
        cm = u_ref[bi, :, MB_INNER + MB_BC_WIDTH:]

        xdt = dt_ref[bi] + dtb_ref[...]
        dt = jnp.maximum(xdt, 0.0) + jnp.log(1.0 + jnp.exp(-jnp.abs(xdt)))
        dta = dt * (-jnp.exp(alog_ref[...]))
        cum_f = _dot2(tril_ref[...], dta)
        cum_b = _dot2(triu_ref[...], dta)
        cum = jnp.where(fwd, cum_f, cum_b)
        cum_end = jnp.where(fwd, cum[L - 1:L, :], cum[0:1, :])
        e_in = _dot(jnp.exp(cum).astype(BF16), expand)
        e_out = _dot((jnp.exp(cum_end - cum) * dt).astype(BF16), expand)
        e_end = jnp.where(fwd, e_in[L - 1:L, :], e_in[0:1, :])
        x_out = (xs * e_out).astype(BF16)

        y_parts = []
        for g in range(MB_GROUPS):
            gsl = slice(g * MB_GROUP_WIDTH, (g + 1) * MB_GROUP_WIDTH)
            nsl = slice(g * MB_STATE, (g + 1) * MB_STATE)
            st = st_s[bi, :, gsl]
            y_parts.append(_dot(cm[:, nsl], st.astype(BF16)) * e_in[:, gsl])
            st_s[bi, :, gsl] = st * e_end[:, gsl] + _dot_tn(bm[:, nsl], x_out[:, gsl])
        per_seq.append((xs16, xs, bm, cm, dt, cum_f, cum_b, jnp.concatenate(y_parts, axis=1)))

    rows = pl.ds(row0, L)

    @pl.when(p == 0)
    def _():
        for bi in range(MB_BATCH):
            acc_s[bi, rows, :] = per_seq[bi][-1]

    @pl.when(p == 1)
    def _():
        trow = lax.broadcasted_iota(jnp.int32, (L, L), 0)
        tcol = lax.broadcasted_iota(jnp.int32, (L, L), 1)
        lower = tcol <= trow
        diagonal = tcol == trow
        lane = lax.broadcasted_iota(jnp.int32, (L, LANES), 1)
        first_half = lane < MB_HEAD_DIM
        for bi in range(MB_BATCH):
            xs16, xs, bm, cm, dt, cum_f, cum_b, y_inter = per_seq[bi]
            log_dt = jnp.log(dt)
            col_f = cum_f * LOG2E
            col_b = cum_b * LOG2E
            row_f_t = ((cum_f - log_dt) * LOG2E).T
            row_b_t = ((cum_b - log_dt) * LOG2E).T
            dt_t = dt.T
            pieces = []
            for g in range(MB_GROUPS):
                nsl = slice(g * MB_STATE, (g + 1) * MB_STATE)
                cb = _dot_nt(cm[:, nsl], bm[:, nsl])
                for hp in range(MB_HEADS_PER_GROUP // 2):
                    ys = []
                    for k in range(2):
                        h = g * MB_HEADS_PER_GROUP + 2 * hp + k
                        hb = MB_HEADS + h
                        seg_f = col_f[:, h:h + 1] - row_f_t[h:h + 1, :]
                        seg_b = col_b[:, hb:hb + 1] - row_b_t[hb:hb + 1, :]
                        e = jnp.exp2(jnp.where(lower, seg_f, seg_b))
                        m = (cb * (e + jnp.where(diagonal, dt_t[hb:hb + 1, :], 0.0))).astype(BF16)
                        c0 = (h // 2) * LANES
                        ys.append(_dot(m, xs16[:, c0:c0 + LANES]))
                    pieces.append(jnp.where(first_half, ys[0], ys[1]))
            y = jnp.concatenate(pieces, axis=1) + y_inter + acc_s[bi, rows, :]
            y = (y + dexp_ref[...] * xs) * z_ref[bi].astype(F32)
            for g in range(MB_GROUPS):
                gsl = slice(g * MB_GROUP_WIDTH, (g + 1) * MB_GROUP_WIDTH)
                yg = y[:, gsl]
                ms = jnp.mean(yg * yg, axis=-1, keepdims=True)
                o_ref[bi, :, gsl] = (yg * lax.rsqrt(ms + RMS_EPS) * nw_ref[:, gsl]).astype(o_ref.dtype)


def _ssd(u, dt_raw, z, dt_bias_row, a_log_row, d_exp, norm_w, tril, triu, expand, nct):
    bsz, s, _ = u.shape
    nt = s // MB_CHUNK
    tile = lambda p, v: _visit_to_tile(p, v, nct, nt)
    assert bsz % MB_BATCH == 0
    return pl.pallas_call(
        functools.partial(_ssd_kernel, nct=nct, nt=nt),
        grid=(bsz // MB_BATCH, 2, nt),
        in_specs=[
            pl.BlockSpec((MB_BATCH, MB_CHUNK, MB_CONV_DIM), lambda b, p, v: (b, tile(p, v), 0)),
            pl.BlockSpec((MB_BATCH, MB_CHUNK, LANES), lambda b, p, v: (b, tile(p, v), 0)),
            pl.BlockSpec((MB_BATCH, MB_CHUNK, MB_INNER), lambda b, p, v: (b, tile(p, v) * p, 0)),
            pl.BlockSpec((1, LANES), lambda b, p, v: (0, 0)),
            pl.BlockSpec((1, LANES), lambda b, p, v: (0, 0)),
            pl.BlockSpec((1, MB_INNER), lambda b, p, v: (0, 0)),
            pl.BlockSpec((1, MB_INNER), lambda b, p, v: (0, 0)),
            pl.BlockSpec((MB_CHUNK, MB_CHUNK), lambda b, p, v: (0, 0)),
            pl.BlockSpec((MB_CHUNK, MB_CHUNK), lambda b, p, v: (0, 0)),
            pl.BlockSpec((1, LANES, MB_INNER), lambda b, p, v: (p, 0, 0)),
        ],
        out_specs=pl.BlockSpec((MB_BATCH, MB_CHUNK, MB_INNER), lambda b, p, v: (b, tile(p, v) * p, 0)),
        out_shape=jax.ShapeDtypeStruct((bsz, s, MB_INNER), BF16),
        scratch_shapes=[
            pltpu.VMEM((MB_BATCH, MB_STATE, MB_INNER), F32),
            pltpu.VMEM((MB_BATCH, s, MB_INNER), F32),
        ],
        compiler_params=_cparams("parallel", "arbitrary", "arbitrary"),
        name="ssd_bidir",
    )(u, dt_raw, z, dt_bias_row, a_log_row, d_exp, norm_w, tril, triu, expand)


def _merge_kernel(x_ref, ohg_ref, oat_ref, omb_ref, gt_ref, g1_ref, whg_ref, wat_ref, wmb_ref,
                  wout_ref, lng_ref, lnb_ref, o_ref, *, alpha):
    gates = gt_ref[0].astype(F32)
    y = (gates[:, :D_MODEL] * _dot(ohg_ref[0], whg_ref[...])
         + gates[:, D_MODEL:2 * D_MODEL] * _dot(oat_ref[0], wat_ref[...])
         + gates[:, 2 * D_MODEL:] * _dot(omb_ref[0], wmb_ref[...]))
    y = _dot(y.astype(BF16), wout_ref[...])
    o_ref[0] = _layer_norm(alpha * x_ref[0] + g1_ref[0] * y, lng_ref[...], lnb_ref[...])


def _merge(xs, o_hg, o_at, o_mb, gates, modsel, w_hg, w_at, w_mb, w_out, ln_g, ln_b, nct, alpha, skip):
    bsz, s, _ = xs.shape
    row = lambda width: pl.BlockSpec((1, ROW_TILE, width), lambda b, i: (b, i + skip, 0))
    return pl.pallas_call(
        functools.partial(_merge_kernel, alpha=alpha),
        grid=(bsz, s // ROW_TILE - skip),
        in_specs=[row(D_MODEL), row(HG_WIDTH), row(AT_WIDTH), row(MB_INNER), row(3 * D_MODEL),
                  _mod_spec(2, nct - skip),
                  _resident(w_hg.shape), _resident(w_at.shape), _resident(w_mb.shape),
                  _resident(w_out.shape), _resident((1, D_MODEL)), _resident((1, D_MODEL))],
        out_specs=_row_spec(D_MODEL),
        out_shape=jax.ShapeDtypeStruct((bsz, s - skip * ROW_TILE, D_MODEL), F32),
        compiler_params=_cparams("parallel", "arbitrary"),
        name="merge_out_ln",
    )(xs, o_hg, o_at, o_mb, gates, modsel, w_hg, w_at, w_mb, w_out,
      ln_g.reshape(1, -1), ln_b.reshape(1, -1))


def _ffn_kernel(x_ref, sc_ref, sh_ref, g2_ref, win_ref, wout_ref, lng_ref, lnb_ref, o_ref, *, alpha):
    x = x_ref[0]
    h = x * (1.0 + sc_ref[0]) + sh_ref[0]
    gu = _dot(h.astype(BF16), win_ref[...])
    a = _silu(gu[:, :FFN_HIDDEN]) * gu[:, FFN_HIDDEN:]
    y = _dot(a.astype(BF16), wout_ref[...])
    o_ref[0] = _layer_norm(alpha * x + g2_ref[0] * y, lng_ref[...], lnb_ref[...])


def _ffn(xs, modsel, w_in, w_out, ln_g, ln_b, nct, alpha):
    bsz, s, _ = xs.shape
    row = pl.BlockSpec((1, ROW_TILE, D_MODEL), lambda b, i: (b, i, 0))
    return pl.pallas_call(
        functools.partial(_ffn_kernel, alpha=alpha),
        grid=(bsz, s // ROW_TILE),
        in_specs=[row, _mod_spec(4, nct), _mod_spec(3, nct), _mod_spec(5, nct),
                  _resident(w_in.shape), _resident(w_out.shape),
                  _resident((1, D_MODEL)), _resident((1, D_MODEL))],
        out_specs=row,
        out_shape=jax.ShapeDtypeStruct(xs.shape, F32),
        compiler_params=_cparams("parallel", "arbitrary"),
        name="swiglu_ffn_ln",
    )(xs, modsel, modsel, modsel, w_in, w_out, ln_g.reshape(1, -1), ln_b.reshape(1, -1))


def _rope_tables(seq, ctx_len):
    rows = seq // GRID_W
    row, col = jnp.meshgrid(jnp.arange(rows, dtype=F32), jnp.arange(GRID_W, dtype=F32), indexing="ij")
    n_pairs = AT_HEAD_DIM // 4
    inv_freq = ROPE_THETA ** (-jnp.arange(n_pairs, dtype=F32) / n_pairs)
    ang = jnp.concatenate([row.reshape(-1, 1) * inv_freq, col.reshape(-1, 1) * inv_freq], axis=-1)
    cos = jnp.repeat(jnp.cos(ang), 2, axis=-1)
    sin = jnp.repeat(jnp.sin(ang), 2, axis=-1) * jnp.tile(jnp.array([-1.0, 1.0], F32), AT_HEAD_DIM // 2)
    cos = jnp.concatenate([jnp.ones((ctx_len, AT_HEAD_DIM), F32), cos], axis=0)
    sin = jnp.concatenate([jnp.zeros((ctx_len, AT_HEAD_DIM), F32), sin], axis=0)
    return cos, sin


def _block_tri(n, blk):
    r = jnp.arange(n)
    same = (r[:, None] // blk) == (r[None, :] // blk)
    lower = (same & (r[None, :] <= r[:, None])).astype(BF16)
    upper = (same & (r[None, :] >= r[:, None])).astype(BF16)
    return lower, upper


def _head_expand():
    lane = jnp.arange(LANES)[:, None]
    head = jnp.arange(MB_INNER)[None, :] // MB_HEAD_DIM
    fwd = (lane == head).astype(BF16)
    bwd = (lane == head + MB_HEADS).astype(BF16)
    return jnp.stack([bwd, fwd])


def _pad_lanes(row):
    return jnp.pad(row.reshape(1, -1), ((0, 0), (0, LANES - row.size)))


def kernel(x, c, ctx, c_ctx, w_mod, b_mod, w_in, hg_lb, hg_gnorm, at_qnorm, at_knorm,
           mb_conv_w, mb_conv_b, mb_dt_bias, mb_a_log, mb_d, mb_norm,
           w_br_hg, w_br_at, w_br_mb, w_out, ln1_g, ln1_b, w_ffn_in, w_ffn_out, ln2_g, ln2_b):
    bsz, seq, _ = x.shape
    ctx_len = ctx.shape[1]
    depth = w_mod.shape[0]
    assert seq % ROW_TILE == 0 and ctx_len % ROW_TILE == 0 and seq % GRID_W == 0
    assert bsz + 1 <= MODS_ROWS
    alpha = (2 * depth) ** 0.25
    nct = ctx_len // ROW_TILE
    nct_mb = ctx_len // MB_CHUNK

    cos, sin = _rope_tables(seq, ctx_len)
    hg_tril, hg_triu = _block_tri(ROW_TILE, HG_SUB)
    hg_tri = jnp.stack([hg_triu, hg_tril])
    mb_tril, mb_triu = _block_tri(MB_CHUNK, MB_CHUNK)
    expand = _head_expand()

    lbs = _lower_bounds(hg_lb)
    cond = jnp.concatenate([c, c_ctx[None, :], jnp.zeros((MODS_ROWS - bsz - 1, D_MODEL), F32)], axis=0)
    mods = _modulations(cond, w_mod.astype(BF16), b_mod)
    ctx_rows = jnp.broadcast_to(mods[:, bsz:bsz + 1, :], (depth, bsz, mods.shape[-1]))
    modsel_all = jnp.stack([ctx_rows, mods[:, :bsz, :]], axis=2).reshape(depth, 2 * bsz, 1, -1)

    splits = [0]
    for width in IN_SIZES:
        splits.append(splits[-1] + width)
    o_hg, o_at, o_z, o_xbc, o_dt, o_gt = splits[0], splits[5], splits[8], splits[9], splits[10], splits[11]

    xs = jnp.concatenate([ctx, x], axis=1)
    for l in range(depth):
        modsel = modsel_all[l]
        w_cols = lambda a, b: w_in[l, :, a:b].astype(BF16)
        w_dt_in = jnp.pad(w_cols(o_dt, o_gt), ((0, 0), (0, LANES - (o_gt - o_dt))))

        qs, lf, kk, vals, hg_gate = _inproj_hg(xs, modsel, w_cols(o_hg, o_at), lbs[l], nct)
        q, k, v, z_act, gt_act, dt_raw, u = _inproj_rest(
            xs, modsel, w_cols(o_at, o_z), w_cols(o_z, o_xbc), w_cols(o_gt, splits[-1]), w_dt_in,
            w_cols(o_xbc, o_dt), cos, sin, at_qnorm[l], at_knorm[l], mb_conv_w[l], mb_conv_b[l], nct)

        out_hg = _hgrn2(qs, lf, kk, vals, hg_gate, hg_gnorm[l], hg_tri, nct)
        out_at = _attention(q, k, v, nct, ctx_len)
        out_mb = _ssd(u, dt_raw, z_act, _pad_lanes(mb_dt_bias[l]), _pad_lanes(mb_a_log[l]),
                      jnp.repeat(mb_d[l], MB_HEAD_DIM).reshape(1, -1), mb_norm[l].reshape(1, -1),
                      mb_tril, mb_triu, expand, nct_mb)

        skip = nct if l == depth - 1 else 0
        xs = _merge(xs, out_hg, out_at, out_mb, gt_act, modsel,
                    w_br_hg[l].astype(BF16), w_br_at[l].astype(BF16), w_br_mb[l].astype(BF16),
                    w_out[l].astype(BF16), ln1_g[l], ln1_b[l], nct, alpha, skip)
        xs = _ffn(xs, modsel, w_ffn_in[l].astype(BF16), w_ffn_out[l].astype(BF16),
                  ln2_g[l], ln2_b[l], nct - skip, alpha)
    return xs
```
